```python
import math
import jax, jax.numpy as jnp
from jax import lax
import numpy as np

D_MODEL = 1024
BATCH = 8
SEQ = 2048
DEPTH = 4

D_MIX = D_MODEL
D_ATTN = D_MIX // 2
D_LRU = D_MIX - D_ATTN
HEAD_DIM = 64
V_DIM = 2 * HEAD_DIM
N_ATTN_HEADS = D_ATTN // V_DIM
N_LRU_BLOCKS = 8
LRU_BLOCK = D_LRU // N_LRU_BLOCKS
CONV_WIDTH = 4
CONV_PAD = (CONV_WIDTH // 2, CONV_WIDTH - 1 - CONV_WIDTH // 2)
LRU_C = 8.0
MIX_IN = 3 * D_ATTN + 2 * D_LRU
D_FF = ((8 * D_MODEL // 3 + 127) // 128) * 128
ROPE_THETA = 10000.0
Q_BLOCK = 128
EPS = 1e-6
N_SUB = 3

kernel_name = "hybrid_diffattn_rglru_macaron_encoder"


def rms_norm(x, g):
    xf = x.astype(jnp.float32)
    y = xf * lax.rsqrt(jnp.mean(xf * xf, axis=-1, keepdims=True) + EPS)
    return (y * g.astype(jnp.float32)).astype(x.dtype)


def rope_tables(positions):
    inv = ROPE_THETA ** (-jnp.arange(0, HEAD_DIM, 2, dtype=jnp.float32) / HEAD_DIM)
    ang = positions.astype(jnp.float32)[..., None] * inv
    return jnp.cos(ang), jnp.sin(ang)


def apply_rope(t, cos, sin):
    cos = cos[:, :, None, None, :]
    sin = sin[:, :, None, None, :]
    t1, t2 = jnp.split(t.astype(jnp.float32), 2, axis=-1)
    out = jnp.concatenate([t1 * cos - t2 * sin, t2 * cos + t1 * sin], axis=-1)
    return out.astype(t.dtype)


def diff_attention(q, k, v, lam, subln_g, lambda_init, cos, sin):
    B, S = q.shape[0], q.shape[1]
    q = apply_rope(q, cos, sin) * (HEAD_DIM ** -0.5)
    k = apply_rope(k, cos, sin)
    nb = S // Q_BLOCK
    qb = q.reshape(B, nb, Q_BLOCK, N_ATTN_HEADS, 2, HEAD_DIM).transpose(1, 0, 2, 3, 4, 5)

    def block(qblk):
        s = jnp.einsum('bqhmd,bkhmd->bhmqk', qblk, k, preferred_element_type=jnp.float32)
        p = jax.nn.softmax(s, axis=-1)
        w = p[:, :, 0] - lam * p[:, :, 1]
        return jnp.einsum('bhqk,bkhe->bqhe', w.astype(v.dtype), v)

    o = lax.map(block, qb)
    o = o.transpose(1, 0, 2, 3, 4).reshape(B, S, N_ATTN_HEADS, V_DIM)
    o = rms_norm(o, subln_g) * (1.0 - lambda_init)
    return o.reshape(B, S, D_ATTN)


def _lin_combine(e1, e2):
    a1, b1 = e1
    a2, b2 = e2
    return (a1 * a2, a2 * b1 + b2)


def rg_lru(xc, w_gate, b_gate, lam, reverse):
    B, S = xc.shape[0], xc.shape[1]
    xb = xc.reshape(B, S, N_LRU_BLOCKS, LRU_BLOCK)
    g = jnp.einsum('bsni,gnij->gbsnj', xb, w_gate).reshape(2, B, S, D_LRU)
    g = jax.nn.sigmoid(g.astype(jnp.float32) + b_gate.astype(jnp.float32)[:, None, None, :])
    r, i = g[0], g[1]
    log_a = -LRU_C * r * jax.nn.softplus(-lam.astype(jnp.float32))
    a = jnp.exp(log_a)
    mult = jnp.sqrt(-jnp.expm1(2.0 * log_a))
    b = mult * i * xc.astype(jnp.float32)
    _, h = lax.associative_scan(_lin_combine, (a, b), axis=1, reverse=reverse)
    return h


def recurrent_group(xr, yr, conv_w, conv_b, w_gate, b_gate, lam):
    xc = lax.conv_general_dilated(
        xr, conv_w[:, None, :].astype(xr.dtype), window_strides=(1,), padding=[CONV_PAD],
        dimension_numbers=('NWC', 'WIO', 'NWC'), feature_group_count=D_LRU) + conv_b
    h = rg_lru(xc, w_gate[0], b_gate[0], lam[0], False) + rg_lru(xc, w_gate[1], b_gate[1], lam[1], True)
    return (h * jax.nn.gelu(yr.astype(jnp.float32))).astype(xr.dtype)


def swiglu(h, w_in, w_out):
    gu = h @ w_in
    g, u = jnp.split(gu, 2, axis=-1)
    return (jax.nn.silu(g) * u) @ w_out


def setup_inputs(seed: int = 0) -> dict:
    key = jax.random.key(seed)
    ks = jax.random.split(key, 20)
    f32 = jnp.float32
    nrm = lambda k, shape, s: jax.random.normal(k, shape, f32) * s
    u = jax.random.uniform(ks[18], (DEPTH, 2, D_LRU), f32, 0.9, 0.999)
    s = u ** (1.0 / LRU_C)
    lru_lambda = jnp.log(s) - jnp.log1p(-s)
    return {
        "x": nrm(ks[0], (BATCH, SEQ, D_MODEL), 1.0),
        "c": nrm(ks[1], (BATCH, D_MODEL), 1.0),
        "positions": jnp.broadcast_to(jnp.arange(SEQ, dtype=jnp.int32), (BATCH, SEQ)),
        "w_ada": nrm(ks[2], (DEPTH, D_MODEL, N_SUB * 3 * D_MODEL), 0.5 * D_MODEL ** -0.5),
        "b_ada": nrm(ks[3], (DEPTH, N_SUB * 3 * D_MODEL), 0.02),
        "norm_g": 1.0 + nrm(ks[4], (DEPTH, 2 * N_SUB, D_MODEL), 0.05),
        "ffn1_w_in": nrm(ks[5], (DEPTH, D_MODEL, 2 * D_FF), D_MODEL ** -0.5),
        "ffn1_w_out": nrm(ks[6], (DEPTH, D_FF, D_MODEL), D_FF ** -0.5),
        "ffn2_w_in": nrm(ks[7], (DEPTH, D_MODEL, 2 * D_FF), D_MODEL ** -0.5),
        "ffn2_w_out": nrm(ks[8], (DEPTH, D_FF, D_MODEL), D_FF ** -0.5),
        "w_mix_in": nrm(ks[9], (DEPTH, D_MODEL, MIX_IN), D_MODEL ** -0.5),
        "w_mix_out": nrm(ks[10], (DEPTH, D_MIX, D_MODEL), D_MIX ** -0.5),
        "lambda_q": nrm(ks[11], (DEPTH, 2, HEAD_DIM), 0.1),
        "lambda_k": nrm(ks[12], (DEPTH, 2, HEAD_DIM), 0.1),
        "subln_g": 1.0 + nrm(ks[13], (DEPTH, V_DIM), 0.05),
        "conv_w": nrm(ks[14], (DEPTH, CONV_WIDTH, D_LRU), CONV_WIDTH ** -0.5),
        "conv_b": nrm(ks[15], (DEPTH, D_LRU), 0.02),
        "lru_w_gate": nrm(ks[16], (DEPTH, 2, 2, N_LRU_BLOCKS, LRU_BLOCK, LRU_BLOCK), LRU_BLOCK ** -0.5),
        "lru_b_gate": nrm(ks[17], (DEPTH, 2, 2, D_LRU), 0.1),
        "lru_lambda": lru_lambda,
    }


def reference(x, c, positions, w_ada, b_ada, norm_g, ffn1_w_in, ffn1_w_out, ffn2_w_in, ffn2_w_out,
              w_mix_in, w_mix_out, lambda_q, lambda_k, subln_g, conv_w, conv_b,
              lru_w_gate, lru_b_gate, lru_lambda):
    B, S, D = x.shape
    cos, sin = rope_tables(positions)
    cond = jax.nn.silu(c)
    split_pts = [D_ATTN, 2 * D_ATTN, 3 * D_ATTN, 3 * D_ATTN + D_LRU]

    for l in range(DEPTH):
        ada = (cond @ w_ada[l] + b_ada[l]).reshape(B, N_SUB, 3, D)

        def sublayer(h_res, j, fn, res_w):
            shift = ada[:, j, 0][:, None, :]
            scale = ada[:, j, 1][:, None, :]
            gate = ada[:, j, 2][:, None, :]
            h = rms_norm(h_res, norm_g[l, 2 * j]) * (1.0 + scale) + shift
            y = rms_norm(fn(h), norm_g[l, 2 * j + 1])
            return h_res + res_w * gate * y

        lambda_init = 0.8 - 0.6 * math.exp(-0.3 * l)
        lam = (jnp.exp(jnp.sum(lambda_q[l, 0].astype(jnp.float32) * lambda_k[l, 0].astype(jnp.float32)))
               - jnp.exp(jnp.sum(lambda_q[l, 1].astype(jnp.float32) * lambda_k[l, 1].astype(jnp.float32)))
               + lambda_init)

        def mixer(h):
            z = h @ w_mix_in[l]
            q, k, v, xr, yr = jnp.split(z, split_pts, axis=-1)
            q = q.reshape(B, S, N_ATTN_HEADS, 2, HEAD_DIM)
            k = k.reshape(B, S, N_ATTN_HEADS, 2, HEAD_DIM)
            v = v.reshape(B, S, N_ATTN_HEADS, V_DIM)
            attn = diff_attention(q, k, v, lam, subln_g[l], lambda_init, cos, sin)
            rec = recurrent_group(xr, yr, conv_w[l], conv_b[l], lru_w_gate[l], lru_b_gate[l], lru_lambda[l])
            return jnp.concatenate([attn, rec], axis=-1) @ w_mix_out[l]

        x = sublayer(x, 0, lambda h: swiglu(h, ffn1_w_in[l], ffn1_w_out[l]), 0.5)
        x = sublayer(x, 1, mixer, 1.0)
        x = sublayer(x, 2, lambda h: swiglu(h, ffn2_w_in[l], ffn2_w_out[l]), 0.5)
    return x
```

```python
import functools
import math

import jax
import jax.numpy as jnp
from jax import lax
from jax.experimental import pallas as pl
from jax.experimental.pallas import tpu as pltpu

D_MODEL = 1024
D_ATTN = 512
D_LRU = 512
HEAD_DIM = 64
V_DIM = 128
N_ATTN_HEADS = 4
N_LRU_BLOCKS = 8
LRU_BLOCK = 64
CONV_WIDTH = 4
LRU_C = 8.0
MIX_IN = 3 * D_ATTN + 2 * D_LRU
D_FF = 2816
ROPE_THETA = 10000.0
EPS = 1e-6
N_SUB = 3

LANES = 128
MXU_DIM = 256
VMEM_LIMIT_BYTES = 56 * 1024 * 1024

TOKEN_TILE = 1024
Q_TILE = 512
FF_CHUNK = MXU_DIM
ADA_COLS = 2304

SCAN_SEG = 256
SCAN_NSEG = 8
SCAN_PITCH = 264
LANE_GROUPS = D_LRU // LANES

_F32 = jnp.float32
_BF16 = jnp.bfloat16


def _params(*sem):
    return pltpu.CompilerParams(dimension_semantics=sem, vmem_limit_bytes=VMEM_LIMIT_BYTES)


def _rms(x):
    return lax.rsqrt(jnp.mean(x * x, axis=-1, keepdims=True) + EPS)


def _ada_rows(ada_ref, j):
    shift = ada_ref[0, 3 * j:3 * j + 1, :]
    scale = ada_ref[0, 3 * j + 1:3 * j + 2, :]
    gate = ada_ref[0, 3 * j + 2:3 * j + 3, :]
    return shift, scale, gate


def _prenorm(x, g_pre, scale, shift):
    return (x * _rms(x)) * (g_pre * (1.0 + scale)) + shift


def _ada_kernel(c_ref, w_ref, b_ref, o_ref):
    c = c_ref[...]
    cond = (c * jax.nn.sigmoid(c)).astype(_BF16)
    w = w_ref[0].astype(_BF16)
    o_ref[0] = jnp.dot(cond, w, preferred_element_type=_F32) + b_ref[0]


def _ada_all_layers(c, w_ada, b_ada):
    depth, d, n = w_ada.shape
    b = c.shape[0]
    return pl.pallas_call(
        _ada_kernel,
        grid=(depth, n // ADA_COLS),
        in_specs=[
            pl.BlockSpec((b, d), lambda l, i: (0, 0)),
            pl.BlockSpec((1, d, ADA_COLS), lambda l, i: (l, 0, i)),
            pl.BlockSpec((1, 1, ADA_COLS), lambda l, i: (l, 0, i)),
        ],
        out_specs=pl.BlockSpec((1, b, ADA_COLS), lambda l, i: (l, 0, i)),
        out_shape=jax.ShapeDtypeStruct((depth, b, n), _F32),
        compiler_params=_params("parallel", "parallel"),
    )(c, w_ada, b_ada.reshape(depth, 1, n))


def _rope_kernel(pos_ref, inv_ref, sign_ref, cos_ref, sin_ref):
    ang = pos_ref[...].astype(_F32) * inv_ref[...]
    cos_ref[...] = jnp.cos(ang)
    sin_ref[...] = jnp.sin(ang) * sign_ref[...]


def _rope_tables(positions):
    n = positions.size
    inv = ROPE_THETA ** (-jnp.arange(0, HEAD_DIM, 2, dtype=_F32) / HEAD_DIM)
    reps = LANES // (HEAD_DIM // 2)
    inv_row = jnp.tile(inv, reps).reshape(1, LANES)
    half = jnp.concatenate([-jnp.ones((HEAD_DIM // 2,), _F32), jnp.ones((HEAD_DIM // 2,), _F32)])
    sign_row = jnp.tile(half, LANES // HEAD_DIM).reshape(1, LANES)
    row = pl.BlockSpec((1, LANES), lambda i: (0, 0))
    tab = pl.BlockSpec((TOKEN_TILE, LANES), lambda i: (i, 0))
    return pl.pallas_call(
        _rope_kernel,
        grid=(n // TOKEN_TILE,),
        in_specs=[pl.BlockSpec((TOKEN_TILE, 1), lambda i: (i, 0)), row, row],
        out_specs=[tab, tab],
        out_shape=[jax.ShapeDtypeStruct((n, LANES), _F32)] * 2,
        compiler_params=_params("parallel"),
    )(positions.reshape(n, 1), inv_row, sign_row)


def _ffn_kernel(x_ref, ada_ref, g_ref, win_ref, wout_ref, o_ref, act_ref, *, j, res_w):
    x = x_ref[...]
    shift, scale, gate = _ada_rows(ada_ref, j)
    h = _prenorm(x, g_ref[2 * j:2 * j + 1, :], scale, shift).astype(_BF16)
    for c in range(D_FF // FF_CHUNK):
        lo = c * FF_CHUNK
        g = jnp.dot(h, win_ref[:, lo:lo + FF_CHUNK], preferred_element_type=_F32)
        u = jnp.dot(h, win_ref[:, D_FF + lo:D_FF + lo + FF_CHUNK], preferred_element_type=_F32)
        act_ref[:, lo:lo + FF_CHUNK] = (g * jax.nn.sigmoid(g) * u).astype(_BF16)
    y = jnp.dot(act_ref[...], wout_ref[...], preferred_element_type=_F32)
    y = (y * _rms(y)) * g_ref[2 * j + 1:2 * j + 2, :]
    o_ref[...] = x + (res_w * gate) * y


def _ffn_sublayer(x2, ada_l, g_l, w_in, w_out, *, j, res_w, seq):
    n, d = x2.shape
    tiles_per_seq = seq // TOKEN_TILE
    resident = dict(pipeline_mode=pl.Buffered(1))
    return pl.pallas_call(
        functools.partial(_ffn_kernel, j=j, res_w=res_w),
        grid=(n // TOKEN_TILE,),
        in_specs=[
            pl.BlockSpec((TOKEN_TILE, d), lambda i: (i, 0)),
            pl.BlockSpec((1, 3 * N_SUB, d), lambda i: (i // tiles_per_seq, 0, 0)),
            pl.BlockSpec((2 * N_SUB, d), lambda i: (0, 0)),
            pl.BlockSpec((d, 2 * D_FF), lambda i: (0, 0), **resident),
            pl.BlockSpec((D_FF, d), lambda i: (0, 0), **resident),
        ],
        out_specs=pl.BlockSpec((TOKEN_TILE, d), lambda i: (i, 0)),
        out_shape=jax.ShapeDtypeStruct((n, d), _F32),
        scratch_shapes=[pltpu.VMEM((TOKEN_TILE, D_FF), _BF16)],
        compiler_params=_params("parallel"),
    )(x2, ada_l, g_l, w_in, w_out)


def _rope(t, cos, sin_signed):
    lane = lax.broadcasted_iota(jnp.int32, t.shape, 1)
    first_half = (lane % HEAD_DIM) < (HEAD_DIM // 2)
    partner = jnp.where(first_half,
                        pltpu.roll(t, LANES - HEAD_DIM // 2, axis=1),
                        pltpu.roll(t, HEAD_DIM // 2, axis=1))
    return t * cos + partner * sin_signed


def _mix_in_kernel(x_ref, ada_ref, g_ref, w_ref, cos_ref, sin_ref,
                   q_ref, k_ref, v_ref, xr_ref, yr_ref):
    j = 1
    x = x_ref[...]
    shift, scale, _ = _ada_rows(ada_ref, j)
    h = _prenorm(x, g_ref[2 * j:2 * j + 1, :], scale, shift).astype(_BF16)
    cos = cos_ref[...]
    sin = sin_ref[...]
    q_scale = HEAD_DIM ** -0.5
    for hd in range(N_ATTN_HEADS):
        lo = hd * LANES
        q = jnp.dot(h, w_ref[:, lo:lo + LANES], preferred_element_type=_F32)
        q_ref[:, lo:lo + LANES] = (_rope(q, cos, sin) * q_scale).astype(_BF16)
        k = jnp.dot(h, w_ref[:, D_ATTN + lo:D_ATTN + lo + LANES], preferred_element_type=_F32)
        k_ref[:, lo:lo + LANES] = _rope(k, cos, sin).astype(_BF16)
    v_ref[...] = jnp.dot(h, w_ref[:, 2 * D_ATTN:3 * D_ATTN], preferred_element_type=_F32).astype(_BF16)
    xr_ref[...] = jnp.dot(h, w_ref[:, 3 * D_ATTN:3 * D_ATTN + D_LRU], preferred_element_type=_F32)
    yr_ref[...] = jnp.dot(h, w_ref[:, 3 * D_ATTN + D_LRU:], preferred_element_type=_F32)


def _mix_in(x2, ada_l, g_l, w_in, cos_t, sin_t, *, seq):
    n, d = x2.shape
    tiles_per_seq = seq // TOKEN_TILE
    half = lambda dt: jax.ShapeDtypeStruct((n, D_ATTN), dt)
    col = pl.BlockSpec((TOKEN_TILE, D_ATTN), lambda i: (i, 0))
    tab = pl.BlockSpec((TOKEN_TILE, LANES), lambda i: (i, 0))
    return pl.pallas_call(
        _mix_in_kernel,
        grid=(n // TOKEN_TILE,),
        in_specs=[
            pl.BlockSpec((TOKEN_TILE, d), lambda i: (i, 0)),
            pl.BlockSpec((1, 3 * N_SUB, d), lambda i: (i // tiles_per_seq, 0, 0)),
            pl.BlockSpec((2 * N_SUB, d), lambda i: (0, 0)),
            pl.BlockSpec((d, MIX_IN), lambda i: (0, 0)),
            tab, tab,
        ],
        out_specs=[col] * 5,
        out_shape=[half(_BF16), half(_BF16), half(_BF16), half(_F32), half(_F32)],
        compiler_params=_params("parallel"),
    )(x2, ada_l, g_l, w_in, cos_t, sin_t)


def _attn_kernel(q_ref, k_ref, v_ref, lq_ref, lk_ref, sg_ref, o_ref, *, lambda_init):
    q = q_ref[...]
    k = k_ref[...]
    v = v_ref[...]
    lane = lax.broadcasted_iota(jnp.int32, q.shape, 1)
    first_map = lane < HEAD_DIM
    zero = jnp.zeros_like(q)
    nt = (((1,), (1,)), ((), ()))

    def softmax_pv(qm):
        s = lax.dot_general(qm, k, nt, preferred_element_type=_F32)
        p = jnp.exp(s - jnp.max(s, axis=-1, keepdims=True))
        l = jnp.sum(p, axis=-1, keepdims=True)
        return jnp.dot(p.astype(_BF16), v, preferred_element_type=_F32) * (1.0 / l)

    o1 = softmax_pv(jnp.where(first_map, q, zero))
    o2 = softmax_pv(jnp.where(first_map, zero, q))
    lqk = lq_ref[...] * lk_ref[...]
    lam = (jnp.exp(jnp.sum(lqk[0:1, :], axis=-1, keepdims=True))
           - jnp.exp(jnp.sum(lqk[1:2, :], axis=-1, keepdims=True)) + lambda_init)
    o = o1 - lam * o2
    o = (o * _rms(o)) * sg_ref[...] * (1.0 - lambda_init)
    o_ref[...] = o.astype(o_ref.dtype)


def _attention(q, k, v, lq, lk, sg, *, lambda_init, batch, seq):
    n = q.shape[0]
    q_tiles = seq // Q_TILE
    kv = pl.BlockSpec((seq, V_DIM), lambda b, h, i: (b, h))
    small = lambda shape: pl.BlockSpec(shape, lambda b, h, i: (0, 0))
    return pl.pallas_call(
        functools.partial(_attn_kernel, lambda_init=lambda_init),
        grid=(batch, N_ATTN_HEADS, q_tiles),
        in_specs=[
            pl.BlockSpec((Q_TILE, V_DIM), lambda b, h, i: (b * q_tiles + i, h)),
            kv, kv,
            small((2, HEAD_DIM)), small((2, HEAD_DIM)), small((1, V_DIM)),
        ],
        out_specs=pl.BlockSpec((Q_TILE, V_DIM), lambda b, h, i: (b * q_tiles + i, h)),
        out_shape=jax.ShapeDtypeStruct((n, D_ATTN), _BF16),
        compiler_params=_params("parallel", "parallel", "parallel"),
    )(q, k, v, lq, lk, sg.reshape(1, V_DIM))


def _scan_direction(a_scr, b_scr, *, reverse):
    def seg_rows(j):
        return pl.ds(j, SCAN_NSEG, stride=SCAN_PITCH)

    def step(jj, carry):
        j = (SCAN_SEG - 1 - jj) if reverse else jj
        hs, ps = carry
        new_h, new_p = [], []
        for lg in range(LANE_GROUPS):
            a = a_scr[lg, seg_rows(j), :]
            b = b_scr[lg, seg_rows(j), :]
            h = a * hs[lg] + b
            p = a * ps[lg]
            b_scr[lg, seg_rows(j), :] = h
            a_scr[lg, seg_rows(j), :] = p
            new_h.append(h)
            new_p.append(p)
        return tuple(new_h), tuple(new_p)

    zeros = tuple(jnp.zeros((SCAN_NSEG, LANES), _F32) for _ in range(LANE_GROUPS))
    ones = tuple(jnp.ones((SCAN_NSEG, LANES), _F32) for _ in range(LANE_GROUPS))
    h_end, p_tot = lax.fori_loop(0, SCAN_SEG, step, (zeros, ones), unroll=8)

    order = range(SCAN_NSEG - 1, -1, -1) if reverse else range(SCAN_NSEG)
    for lg in range(LANE_GROUPS):
        carry_in = jnp.zeros((1, LANES), _F32)
        for s in order:
            rows = pl.ds(s * SCAN_PITCH, SCAN_SEG)
            b_scr[lg, rows, :] = b_scr[lg, rows, :] + a_scr[lg, rows, :] * carry_in
            carry_in = p_tot[lg][s:s + 1, :] * carry_in + h_end[lg][s:s + 1, :]


def _lru_kernel(xr_ref, yr_ref, cw_ref, cb_ref, wg_ref, bg_ref, lam_ref, o_ref,
                a_scr, b_scr, h_scr):
    x = xr_ref[...]
    seq = x.shape[0]
    row = lax.broadcasted_iota(jnp.int32, x.shape, 0)
    cw = cw_ref[...]
    xc = cb_ref[...] + cw[2:3, :] * x
    xc = xc + cw[0:1, :] * jnp.where(row >= 2, pltpu.roll(x, 2, axis=0), 0.0)
    xc = xc + cw[1:2, :] * jnp.where(row >= 1, pltpu.roll(x, 1, axis=0), 0.0)
    xc = xc + cw[3:4, :] * jnp.where(row < seq - 1, pltpu.roll(x, seq - 1, axis=0), 0.0)
    xcb = xc.astype(_BF16)

    for d in range(2):
        lam = lam_ref[d:d + 1, :]
        neg_c_softplus = -LRU_C * (jnp.maximum(-lam, 0.0) + jnp.log1p(jnp.exp(-jnp.abs(lam))))
        for half in range(D_LRU // MXU_DIM):
            cols = slice(half * MXU_DIM, (half + 1) * MXU_DIM)
            xh = xcb[:, cols]
            gr = jnp.dot(xh, wg_ref[d, 0, half], preferred_element_type=_F32)
            gi = jnp.dot(xh, wg_ref[d, 1, half], preferred_element_type=_F32)
            r = jax.nn.sigmoid(gr + bg_ref[d, 0:1, cols])
            i = jax.nn.sigmoid(gi + bg_ref[d, 1:2, cols])
            log_a = neg_c_softplus[:, cols] * r
            a = jnp.exp(log_a)
            b = jnp.sqrt(jnp.tanh(-log_a) * (a * a + 1.0)) * i * xc[:, cols]
            for sub in range(MXU_DIM // LANES):
                lg = half * (MXU_DIM // LANES) + sub
                lanes = slice(sub * LANES, (sub + 1) * LANES)
                for s in range(SCAN_NSEG):
                    src = slice(s * SCAN_SEG, (s + 1) * SCAN_SEG)
                    dst = pl.ds(s * SCAN_PITCH, SCAN_SEG)
                    a_scr[lg, dst, :] = a[src, lanes]
                    b_scr[lg, dst, :] = b[src, lanes]
        _scan_direction(a_scr, b_scr, reverse=(d == 1))
        for lg in range(LANE_GROUPS):
            for s in range(SCAN_NSEG):
                dst = (slice(s * SCAN_SEG, (s + 1) * SCAN_SEG), slice(lg * LANES, (lg + 1) * LANES))
                hseg = b_scr[lg, pl.ds(s * SCAN_PITCH, SCAN_SEG), :]
                if d == 0:
                    h_scr[dst] = hseg
                else:
                    h_scr[dst] = h_scr[dst] + hseg
    o_ref[...] = (h_scr[...] * jax.nn.gelu(yr_ref[...])).astype(o_ref.dtype)


def _recurrent_group(xr, yr, conv_w, conv_b, wg_bd, b_gate, lam, *, batch, seq):
    n = xr.shape[0]
    tok = pl.BlockSpec((seq, D_LRU), lambda b: (b, 0))
    scan_buf = pltpu.VMEM((LANE_GROUPS, SCAN_NSEG * SCAN_PITCH, LANES), _F32)
    return pl.pallas_call(
        _lru_kernel,
        grid=(batch,),
        in_specs=[
            tok, tok,
            pl.BlockSpec((CONV_WIDTH, D_LRU), lambda b: (0, 0)),
            pl.BlockSpec((1, D_LRU), lambda b: (0, 0)),
            pl.BlockSpec(wg_bd.shape, lambda b: (0, 0, 0, 0, 0)),
            pl.BlockSpec((2, 2, D_LRU), lambda b: (0, 0, 0)),
            pl.BlockSpec((2, D_LRU), lambda b: (0, 0)),
        ],
        out_specs=tok,
        out_shape=jax.ShapeDtypeStruct((n, D_LRU), _BF16),
        scratch_shapes=[scan_buf, scan_buf, pltpu.VMEM((seq, D_LRU), _F32)],
        compiler_params=_params("parallel"),
    )(xr, yr, conv_w, conv_b.reshape(1, D_LRU), wg_bd, b_gate, lam)


def _block_diag_gates(w_gate):
    per_tile = MXU_DIM // LRU_BLOCK
    w = w_gate.reshape(2, 2, N_LRU_BLOCKS // per_tile, per_tile, LRU_BLOCK, LRU_BLOCK)
    eye = jnp.eye(per_tile, dtype=w.dtype)
    bd = jnp.einsum('dghnij,nm->dghnimj', w, eye)
    return bd.reshape(2, 2, N_LRU_BLOCKS // per_tile, MXU_DIM, MXU_DIM)


def _mix_out_kernel(x_ref, attn_ref, rec_ref, ada_ref, g_ref, w_ref, o_ref):
    j = 1
    _, _, gate = _ada_rows(ada_ref, j)
    y = (jnp.dot(attn_ref[...], w_ref[:D_ATTN, :], preferred_element_type=_F32)
         + jnp.dot(rec_ref[...], w_ref[D_ATTN:, :], preferred_element_type=_F32))
    y = (y * _rms(y)) * g_ref[2 * j + 1:2 * j + 2, :]
    o_ref[...] = x_ref[...] + gate * y


def _mix_out(x2, attn, rec, ada_l, g_l, w_out, *, seq):
    n, d = x2.shape
    tiles_per_seq = seq // TOKEN_TILE
    col = pl.BlockSpec((TOKEN_TILE, D_ATTN), lambda i: (i, 0))
    return pl.pallas_call(
        _mix_out_kernel,
        grid=(n // TOKEN_TILE,),
        in_specs=[
            pl.BlockSpec((TOKEN_TILE, d), lambda i: (i, 0)),
            col, col,
            pl.BlockSpec((1, 3 * N_SUB, d), lambda i: (i // tiles_per_seq, 0, 0)),
            pl.BlockSpec((2 * N_SUB, d), lambda i: (0, 0)),
            pl.BlockSpec((d, d), lambda i: (0, 0)),
        ],
        out_specs=pl.BlockSpec((TOKEN_TILE, d), lambda i: (i, 0)),
        out_shape=jax.ShapeDtypeStruct((n, d), _F32),
        compiler_params=_params("parallel"),
    )(x2, attn, rec, ada_l, g_l, w_out)


def kernel(x, c, positions, w_ada, b_ada, norm_g, ffn1_w_in, ffn1_w_out, ffn2_w_in, ffn2_w_out,
           w_mix_in, w_mix_out, lambda_q, lambda_k, subln_g, conv_w, conv_b,
           lru_w_gate, lru_b_gate, lru_lambda):
    batch, seq, d = x.shape
    depth = w_ada.shape[0]
    assert (d, seq % TOKEN_TILE, seq % Q_TILE) == (D_MODEL, 0, 0)
    assert seq == SCAN_SEG * SCAN_NSEG

    ada = _ada_all_layers(c, w_ada, b_ada).reshape(depth, batch, 3 * N_SUB, d)
    cos_t, sin_t = _rope_tables(positions)
    x2 = x.reshape(batch * seq, d)

    for l in range(depth):
        lambda_init = 0.8 - 0.6 * math.exp(-0.3 * l)
        x2 = _ffn_sublayer(x2, ada[l], norm_g[l], ffn1_w_in[l].astype(_BF16), ffn1_w_out[l].astype(_BF16),
                           j=0, res_w=0.5, seq=seq)
        q, k, v, xr, yr = _mix_in(x2, ada[l], norm_g[l], w_mix_in[l].astype(_BF16), cos_t, sin_t, seq=seq)
        attn = _attention(q, k, v, lambda_q[l], lambda_k[l], subln_g[l],
                          lambda_init=lambda_init, batch=batch, seq=seq)
        rec = _recurrent_group(xr, yr, conv_w[l], conv_b[l],
                               _block_diag_gates(lru_w_gate[l]).astype(_BF16),
                               lru_b_gate[l], lru_lambda[l], batch=batch, seq=seq)
        x2 = _mix_out(x2, attn, rec, ada[l], norm_g[l], w_mix_out[l].astype(_BF16), seq=seq)
        x2 = _ffn_sublayer(x2, ada[l], norm_g[l], ffn2_w_in[l].astype(_BF16), ffn2_w_out[l].astype(_BF16),
                           j=2, res_w=0.5, seq=seq)
    return x2.reshape(batch, seq, d)
```

```python
import functools
import math

import jax
import jax.numpy as jnp
from jax import lax
from jax.experimental import pallas as pl
from jax.experimental.pallas import tpu as pltpu

D_MODEL = 1024
D_ATTN = 512
D_LRU = 512
HEAD_DIM = 64
V_DIM = 128
N_ATTN_HEADS = 4
N_LRU_BLOCKS = 8
LRU_BLOCK = 64
CONV_WIDTH = 4
LRU_C = 8.0
MIX_IN = 3 * D_ATTN + 2 * D_LRU
D_FF = 2816
ROPE_THETA = 10000.0
EPS = 1e-6
N_SUB = 3
LOG2_E = 1.4426950408889634

LANES = 128
SUBLANES = 8
MXU_DIM = 256
VMEM_LIMIT_BYTES = 56 * 1024 * 1024

TOKEN_TILE = 1024
Q_TILE = 512
FF_CHUNK = MXU_DIM
ADA_COLS = 2304

SCAN_SEG = 64
SCAN_NSEG = 32
SCAN_PITCH = 72
LANE_GROUPS = D_LRU // LANES

_F32 = jnp.float32
_BF16 = jnp.bfloat16


def _params(*sem):
    return pltpu.CompilerParams(dimension_semantics=sem, vmem_limit_bytes=VMEM_LIMIT_BYTES)


def _rms(x):
    return lax.rsqrt(jnp.mean(x * x, axis=-1, keepdims=True) + EPS)


def _ada_rows(ada_ref, j):
    shift = ada_ref[0, 3 * j:3 * j + 1, :]
    scale = ada_ref[0, 3 * j + 1:3 * j + 2, :]
    gate = ada_ref[0, 3 * j + 2:3 * j + 3, :]
    return shift, scale, gate


def _prenorm(x, g_pre, scale, shift):
    return (x * _rms(x)) * (g_pre * (1.0 + scale)) + shift


def _ada_kernel(c_ref, w_ref, b_ref, o_ref):
    c = c_ref[...]
    cond = (c * jax.nn.sigmoid(c)).astype(_BF16)
    w = w_ref[0].astype(_BF16)
    o_ref[0] = jnp.dot(cond, w, preferred_element_type=_F32) + b_ref[0]


def _ada_all_layers(c, w_ada, b_ada):
    depth, d, n = w_ada.shape
    b = c.shape[0]
    return pl.pallas_call(
        _ada_kernel,
        grid=(depth, n // ADA_COLS),
        in_specs=[
            pl.BlockSpec((b, d), lambda l, i: (0, 0)),
            pl.BlockSpec((1, d, ADA_COLS), lambda l, i: (l, 0, i)),
            pl.BlockSpec((1, 1, ADA_COLS), lambda l, i: (l, 0, i)),
        ],
        out_specs=pl.BlockSpec((1, b, ADA_COLS), lambda l, i: (l, 0, i)),
        out_shape=jax.ShapeDtypeStruct((depth, b, n), _F32),
        compiler_params=_params("parallel", "parallel"),
    )(c, w_ada, b_ada.reshape(depth, 1, n))


def _rope_kernel(pos_ref, inv_ref, sign_ref, cos_ref, sin_ref):
    ang = pos_ref[...].astype(_F32) * inv_ref[...]
    cos_ref[...] = jnp.cos(ang)
    sin_ref[...] = jnp.sin(ang) * sign_ref[...]


def _rope_tables(positions):
    n = positions.size
    inv = ROPE_THETA ** (-jnp.arange(0, HEAD_DIM, 2, dtype=_F32) / HEAD_DIM)
    reps = LANES // (HEAD_DIM // 2)
    inv_row = jnp.tile(inv, reps).reshape(1, LANES)
    half = jnp.concatenate([-jnp.ones((HEAD_DIM // 2,), _F32), jnp.ones((HEAD_DIM // 2,), _F32)])
    sign_row = jnp.tile(half, LANES // HEAD_DIM).reshape(1, LANES)
    row = pl.BlockSpec((1, LANES), lambda i: (0, 0))
    tab = pl.BlockSpec((TOKEN_TILE, LANES), lambda i: (i, 0))
    return pl.pallas_call(
        _rope_kernel,
        grid=(n // TOKEN_TILE,),
        in_specs=[pl.BlockSpec((TOKEN_TILE, 1), lambda i: (i, 0)), row, row],
        out_specs=[tab, tab],
        out_shape=[jax.ShapeDtypeStruct((n, LANES), _F32)] * 2,
        compiler_params=_params("parallel"),
    )(positions.reshape(n, 1), inv_row, sign_row)


def _ffn_kernel(x_ref, ada_ref, g_ref, win_ref, wout_ref, o_ref, act_ref, *, j, res_w):
    x = x_ref[...]
    shift, scale, gate = _ada_rows(ada_ref, j)
    h = _prenorm(x, g_ref[2 * j:2 * j + 1, :], scale, shift).astype(_BF16)
    for c in range(D_FF // FF_CHUNK):
        lo = c * FF_CHUNK
        g = jnp.dot(h, win_ref[:, lo:lo + FF_CHUNK], preferred_element_type=_F32)
        u = jnp.dot(h, win_ref[:, D_FF + lo:D_FF + lo + FF_CHUNK], preferred_element_type=_F32)
        act_ref[:, lo:lo + FF_CHUNK] = (g * jax.nn.sigmoid(g) * u).astype(_BF16)
    y = jnp.dot(act_ref[...], wout_ref[...], preferred_element_type=_F32)
    y = (y * _rms(y)) * g_ref[2 * j + 1:2 * j + 2, :]
    o_ref[...] = x + (res_w * gate) * y


def _ffn_sublayer(x2, ada_l, g_l, w_in, w_out, *, j, res_w, seq):
    n, d = x2.shape
    tiles_per_seq = seq // TOKEN_TILE
    resident = dict(pipeline_mode=pl.Buffered(1))
    return pl.pallas_call(
        functools.partial(_ffn_kernel, j=j, res_w=res_w),
        grid=(n // TOKEN_TILE,),
        in_specs=[
            pl.BlockSpec((TOKEN_TILE, d), lambda i: (i, 0)),
            pl.BlockSpec((1, 3 * N_SUB, d), lambda i: (i // tiles_per_seq, 0, 0)),
            pl.BlockSpec((2 * N_SUB, d), lambda i: (0, 0)),
            pl.BlockSpec((d, 2 * D_FF), lambda i: (0, 0), **resident),
            pl.BlockSpec((D_FF, d), lambda i: (0, 0), **resident),
        ],
        out_specs=pl.BlockSpec((TOKEN_TILE, d), lambda i: (i, 0)),
        out_shape=jax.ShapeDtypeStruct((n, d), _F32),
        scratch_shapes=[pltpu.VMEM((TOKEN_TILE, D_FF), _BF16)],
        compiler_params=_params("parallel"),
    )(x2, ada_l, g_l, w_in, w_out)


def _rope(t, cos, sin_signed):
    lane = lax.broadcasted_iota(jnp.int32, t.shape, 1)
    first_half = (lane % HEAD_DIM) < (HEAD_DIM // 2)
    partner = jnp.where(first_half,
                        pltpu.roll(t, LANES - HEAD_DIM // 2, axis=1),
                        pltpu.roll(t, HEAD_DIM // 2, axis=1))
    return t * cos + partner * sin_signed


def _mix_in_kernel(x_ref, ada_ref, g_ref, w_ref, cos_ref, sin_ref,
                   q_ref, k_ref, v_ref, xr_ref, yr_ref):
    j = 1
    x = x_ref[...]
    shift, scale, _ = _ada_rows(ada_ref, j)
    h = _prenorm(x, g_ref[2 * j:2 * j + 1, :], scale, shift).astype(_BF16)
    cos = cos_ref[...]
    sin = sin_ref[...]
    q_scale = HEAD_DIM ** -0.5 * LOG2_E
    for hd in range(N_ATTN_HEADS):
        lo = hd * LANES
        q = jnp.dot(h, w_ref[:, lo:lo + LANES], preferred_element_type=_F32)
        q_ref[:, lo:lo + LANES] = (_rope(q, cos, sin) * q_scale).astype(_BF16)
        k = jnp.dot(h, w_ref[:, D_ATTN + lo:D_ATTN + lo + LANES], preferred_element_type=_F32)
        k_ref[:, lo:lo + LANES] = _rope(k, cos, sin).astype(_BF16)
    v_ref[...] = jnp.dot(h, w_ref[:, 2 * D_ATTN:3 * D_ATTN], preferred_element_type=_F32).astype(_BF16)
    xr_ref[...] = jnp.dot(h, w_ref[:, 3 * D_ATTN:3 * D_ATTN + D_LRU], preferred_element_type=_F32)
    yr_ref[...] = jnp.dot(h, w_ref[:, 3 * D_ATTN + D_LRU:], preferred_element_type=_F32)


def _mix_in(x2, ada_l, g_l, w_in, cos_t, sin_t, *, seq):
    n, d = x2.shape
    tiles_per_seq = seq // TOKEN_TILE
    half = lambda dt: jax.ShapeDtypeStruct((n, D_ATTN), dt)
    col = pl.BlockSpec((TOKEN_TILE, D_ATTN), lambda i: (i, 0))
    tab = pl.BlockSpec((TOKEN_TILE, LANES), lambda i: (i, 0))
    return pl.pallas_call(
        _mix_in_kernel,
        grid=(n // TOKEN_TILE,),
        in_specs=[
            pl.BlockSpec((TOKEN_TILE, d), lambda i: (i, 0)),
            pl.BlockSpec((1, 3 * N_SUB, d), lambda i: (i // tiles_per_seq, 0, 0)),
            pl.BlockSpec((2 * N_SUB, d), lambda i: (0, 0)),
            pl.BlockSpec((d, MIX_IN), lambda i: (0, 0)),
            tab, tab,
        ],
        out_specs=[col] * 5,
        out_shape=[half(_BF16), half(_BF16), half(_BF16), half(_F32), half(_F32)],
        compiler_params=_params("parallel"),
    )(x2, ada_l, g_l, w_in, cos_t, sin_t)


def _store_scores(q, k, s_ref):
    lane = lax.broadcasted_iota(jnp.int32, q.shape, 1)
    first_map = lane < HEAD_DIM
    zero = jnp.zeros_like(q)
    nt = (((1,), (1,)), ((), ()))
    s_ref[0] = lax.dot_general(jnp.where(first_map, q, zero), k, nt, preferred_element_type=_F32)
    s_ref[1] = lax.dot_general(jnp.where(first_map, zero, q), k, nt, preferred_element_type=_F32)


def _diff_softmax_pv(s_ref, v_ones, lam, out_gain):
    maps = []
    for m in range(2):
        s = s_ref[m]
        p = jnp.exp2(s - jnp.max(s, axis=-1, keepdims=True))
        pv = jnp.dot(p.astype(_BF16), v_ones, preferred_element_type=_F32)
        maps.append(pv[:, :V_DIM] * (1.0 / pv[:, V_DIM:]))
    o = maps[0] - lam * maps[1]
    return (o * _rms(o)) * out_gain


def _attn_kernel(q0_ref, q1_ref, q2_ref, k_ref, kn_ref, v_ref, lq_ref, lk_ref, sg_ref, o_ref,
                 sa_ref, sb_ref, *, lambda_init):
    @pl.when(pl.program_id(0) == 0)
    def _():
        _store_scores(q0_ref[...], k_ref[...], sa_ref)

    v_ones = jnp.concatenate([v_ref[...], jnp.ones(v_ref.shape, _BF16)], axis=1)
    lqk = lq_ref[...] * lk_ref[...]
    lam = (jnp.exp(jnp.sum(lqk[0:1, :], axis=-1, keepdims=True))
           - jnp.exp(jnp.sum(lqk[1:2, :], axis=-1, keepdims=True)) + lambda_init)
    out_gain = sg_ref[...] * (1.0 - lambda_init)

    _store_scores(q1_ref[...], k_ref[...], sb_ref)
    o_ref[:Q_TILE, :] = _diff_softmax_pv(sa_ref, v_ones, lam, out_gain).astype(o_ref.dtype)
    _store_scores(q2_ref[...], kn_ref[...], sa_ref)
    o_ref[Q_TILE:, :] = _diff_softmax_pv(sb_ref, v_ones, lam, out_gain).astype(o_ref.dtype)


def _attention(q, k, v, lq, lk, sg, *, lambda_init, batch, seq):
    n = q.shape[0]
    q_tiles = seq // Q_TILE
    n_tiles = batch * N_ATTN_HEADS * q_tiles
    assert q_tiles % 2 == 0

    def q_block(tile):
        bh, i = tile // q_tiles, tile % q_tiles
        return (bh // N_ATTN_HEADS) * q_tiles + i, bh % N_ATTN_HEADS

    def kv_block(tile):
        bh = tile // q_tiles
        return bh // N_ATTN_HEADS, bh % N_ATTN_HEADS

    def out_block(step):
        row, col = q_block(2 * step)
        return row // 2, col

    last = n_tiles - 1
    q_spec = lambda tile_of: pl.BlockSpec((Q_TILE, V_DIM), lambda t: q_block(tile_of(t)))
    kv_spec = lambda tile_of: pl.BlockSpec((seq, V_DIM), lambda t: kv_block(tile_of(t)))
    small = lambda shape: pl.BlockSpec(shape, lambda t: (0, 0))
    scores = pltpu.VMEM((2, Q_TILE, seq), _F32)
    return pl.pallas_call(
        functools.partial(_attn_kernel, lambda_init=lambda_init),
        grid=(n_tiles // 2,),
        in_specs=[
            q_spec(lambda t: 2 * t),
            q_spec(lambda t: 2 * t + 1),
            q_spec(lambda t: jnp.minimum(2 * t + 2, last)),
            kv_spec(lambda t: 2 * t),
            kv_spec(lambda t: jnp.minimum(2 * t + 2, last)),
            kv_spec(lambda t: 2 * t),
            small((2, HEAD_DIM)), small((2, HEAD_DIM)), small((1, V_DIM)),
        ],
        out_specs=pl.BlockSpec((2 * Q_TILE, V_DIM), lambda t: out_block(t)),
        out_shape=jax.ShapeDtypeStruct((n, D_ATTN), _BF16),
        scratch_shapes=[scores, scores],
        compiler_params=_params("arbitrary"),
    )(q, q, q, k, k, v, lq, lk, sg.reshape(1, V_DIM))


def _scan_direction(a_scr, b_scr, p_scr, h_scr, *, reverse):
    chains = [(lg, sg) for lg in range(LANE_GROUPS) for sg in range(SCAN_NSEG // SUBLANES)]

    def seg_rows(sg, j):
        return pl.ds(sg * SUBLANES * SCAN_PITCH + j, SUBLANES, stride=SCAN_PITCH)

    def step(jj, carry):
        j = (SCAN_SEG - 1 - jj) if reverse else jj
        hs, ps = carry
        new_h, new_p = [], []
        for c, (lg, sg) in enumerate(chains):
            a = a_scr[lg, seg_rows(sg, j), :]
            b = b_scr[lg, seg_rows(sg, j), :]
            h = a * hs[c] + b
            p = a * ps[c]
            h_scr[lg, seg_rows(sg, j), :] = h
            p_scr[lg, seg_rows(sg, j), :] = p
            new_h.append(h)
            new_p.append(p)
        return tuple(new_h), tuple(new_p)

    zeros = tuple(jnp.zeros((SUBLANES, LANES), _F32) for _ in chains)
    ones = tuple(jnp.ones((SUBLANES, LANES), _F32) for _ in chains)
    h_end, p_tot = lax.fori_loop(0, SCAN_SEG, step, (zeros, ones), unroll=8)

    order = range(SCAN_NSEG - 1, -1, -1) if reverse else range(SCAN_NSEG)
    carries = []
    for lg in range(LANE_GROUPS):
        carry_in = jnp.zeros((1, LANES), _F32)
        per_seg = [None] * SCAN_NSEG
        for s in order:
            c = chains.index((lg, s // SUBLANES))
            r = s % SUBLANES
            per_seg[s] = carry_in
            carry_in = p_tot[c][r:r + 1, :] * carry_in + h_end[c][r:r + 1, :]
        carries.append(per_seg)
    return carries


def _depthwise_conv(x_ref, cw, cb):
    seq = x_ref.shape[0]

    def taps(x, m2, m1, p1):
        return cb + cw[2:3, :] * x + cw[0:1, :] * m2 + cw[1:2, :] * m1 + cw[3:4, :] * p1

    row = lax.broadcasted_iota(jnp.int32, (SUBLANES, x_ref.shape[1]), 0)
    lo, hi = SUBLANES, seq - SUBLANES
    x_top = x_ref[0:lo, :]
    x_bot = x_ref[hi:seq, :]
    return jnp.concatenate([
        taps(x_top,
             jnp.where(row >= 2, pltpu.roll(x_top, 2, axis=0), 0.0),
             jnp.where(row >= 1, pltpu.roll(x_top, 1, axis=0), 0.0),
             x_ref[1:lo + 1, :]),
        taps(x_ref[lo:hi, :], x_ref[lo - 2:hi - 2, :], x_ref[lo - 1:hi - 1, :], x_ref[lo + 1:hi + 1, :]),
        taps(x_bot, x_ref[hi - 2:seq - 2, :], x_ref[hi - 1:seq - 1, :],
             jnp.where(row < SUBLANES - 1, pltpu.roll(x_bot, SUBLANES - 1, axis=0), 0.0)),
    ], axis=0)


def _lru_kernel(xr_ref, yr_ref, cw_ref, cb_ref, wg_ref, bg_ref, lam_ref, o_ref,
                a_scr, b_scr, p_scr, h_scr, sum_scr):
    xc = _depthwise_conv(xr_ref, cw_ref[...], cb_ref[...])
    xcb = xc.astype(_BF16)
    xc_half = 0.5 * xc

    for d in range(2):
        lam = lam_ref[d:d + 1, :]
        half_c = (-0.5 * LRU_C) * (jnp.maximum(-lam, 0.0) + jnp.log1p(jnp.exp(-jnp.abs(lam))))
        for half in range(D_LRU // MXU_DIM):
            cols = slice(half * MXU_DIM, (half + 1) * MXU_DIM)
            xh = xcb[:, cols]
            gr = jnp.dot(xh, wg_ref[d, 0, half], preferred_element_type=_F32)
            gi = jnp.dot(xh, wg_ref[d, 1, half], preferred_element_type=_F32)
            t_r = jnp.tanh(gr + 0.5 * bg_ref[d, 0:1, cols])
            t_i = jnp.tanh(gi + 0.5 * bg_ref[d, 1:2, cols])
            log_a = half_c[:, cols] * t_r + half_c[:, cols]
            a = jnp.exp(log_a)
            one_minus_a2 = jnp.tanh(log_a) * (-1.0 - a * a)
            mult = jnp.where(one_minus_a2 > 0.0, one_minus_a2 * lax.rsqrt(one_minus_a2), 0.0)
            b = mult * ((t_i + 1.0) * xc_half[:, cols])
            for sub in range(MXU_DIM // LANES):
                lg = half * (MXU_DIM // LANES) + sub
                lanes = slice(sub * LANES, (sub + 1) * LANES)
                for s in range(SCAN_NSEG):
                    src = slice(s * SCAN_SEG, (s + 1) * SCAN_SEG)
                    dst = pl.ds(s * SCAN_PITCH, SCAN_SEG)
                    a_scr[lg, dst, :] = a[src, lanes]
                    b_scr[lg, dst, :] = b[src, lanes]
        carries = _scan_direction(a_scr, b_scr, p_scr, h_scr, reverse=(d == 1))
        for lg in range(LANE_GROUPS):
            for s in range(SCAN_NSEG):
                rows = pl.ds(s * SCAN_PITCH, SCAN_SEG)
                dst = (slice(s * SCAN_SEG, (s + 1) * SCAN_SEG), slice(lg * LANES, (lg + 1) * LANES))
                hseg = h_scr[lg, rows, :] + p_scr[lg, rows, :] * carries[lg][s]
                if d == 0:
                    sum_scr[dst] = hseg
                else:
                    sum_scr[dst] = sum_scr[dst] + hseg
    o_ref[...] = (sum_scr[...] * jax.nn.gelu(yr_ref[...])).astype(o_ref.dtype)


def _recurrent_group(xr, yr, conv_w, conv_b, wg_bd, b_gate, lam, *, batch, seq):
    n = xr.shape[0]
    tok = pl.BlockSpec((seq, D_LRU), lambda b: (b, 0))
    scan_buf = pltpu.VMEM((LANE_GROUPS, SCAN_NSEG * SCAN_PITCH, LANES), _F32)
    return pl.pallas_call(
        _lru_kernel,
        grid=(batch,),
        in_specs=[
            tok, tok,
            pl.BlockSpec((CONV_WIDTH, D_LRU), lambda b: (0, 0)),
            pl.BlockSpec((1, D_LRU), lambda b: (0, 0)),
            pl.BlockSpec(wg_bd.shape, lambda b: (0, 0, 0, 0, 0)),
            pl.BlockSpec((2, 2, D_LRU), lambda b: (0, 0, 0)),
            pl.BlockSpec((2, D_LRU), lambda b: (0, 0)),
        ],
        out_specs=tok,
        out_shape=jax.ShapeDtypeStruct((n, D_LRU), _BF16),
        scratch_shapes=[scan_buf] * 4 + [pltpu.VMEM((seq, D_LRU), _F32)],
        compiler_params=_params("parallel"),
    )(xr, yr, conv_w, conv_b.reshape(1, D_LRU), wg_bd, b_gate, lam)


def _block_diag_gates(w_gate):
    per_tile = MXU_DIM // LRU_BLOCK
    w = w_gate.reshape(2, 2, N_LRU_BLOCKS // per_tile, per_tile, LRU_BLOCK, LRU_BLOCK)
    eye = jnp.eye(per_tile, dtype=w.dtype)
    bd = jnp.einsum('dghnij,nm->dghnimj', w, eye)
    return bd.reshape(2, 2, N_LRU_BLOCKS // per_tile, MXU_DIM, MXU_DIM)


def _mix_out_kernel(x_ref, attn_ref, rec_ref, ada_ref, g_ref, w_ref, o_ref):
    j = 1
    _, _, gate = _ada_rows(ada_ref, j)
    y = (jnp.dot(attn_ref[...], w_ref[:D_ATTN, :], preferred_element_type=_F32)
         + jnp.dot(rec_ref[...], w_ref[D_ATTN:, :], preferred_element_type=_F32))
    y = (y * _rms(y)) * g_ref[2 * j + 1:2 * j + 2, :]
    o_ref[...] = x_ref[...] + gate * y


def _mix_out(x2, attn, rec, ada_l, g_l, w_out, *, seq):
    n, d = x2.shape
    tiles_per_seq = seq // TOKEN_TILE
    col = pl.BlockSpec((TOKEN_TILE, D_ATTN), lambda i: (i, 0))
    return pl.pallas_call(
        _mix_out_kernel,
        grid=(n // TOKEN_TILE,),
        in_specs=[
            pl.BlockSpec((TOKEN_TILE, d), lambda i: (i, 0)),
            col, col,
            pl.BlockSpec((1, 3 * N_SUB, d), lambda i: (i // tiles_per_seq, 0, 0)),
            pl.BlockSpec((2 * N_SUB, d), lambda i: (0, 0)),
            pl.BlockSpec((d, d), lambda i: (0, 0)),
        ],
        out_specs=pl.BlockSpec((TOKEN_TILE, d), lambda i: (i, 0)),
        out_shape=jax.ShapeDtypeStruct((n, d), _F32),
        compiler_params=_params("parallel"),
    )(x2, attn, rec, ada_l, g_l, w_out)


def kernel(x, c, positions, w_ada, b_ada, norm_g, ffn1_w_in, ffn1_w_out, ffn2_w_in, ffn2_w_out,
           w_mix_in, w_mix_out, lambda_q, lambda_k, subln_g, conv_w, conv_b,
           lru_w_gate, lru_b_gate, lru_lambda):
    batch, seq, d = x.shape
    depth = w_ada.shape[0]
    assert (d, seq % TOKEN_TILE, seq % Q_TILE) == (D_MODEL, 0, 0)
    assert seq == SCAN_SEG * SCAN_NSEG

    ada = _ada_all_layers(c, w_ada, b_ada).reshape(depth, batch, 3 * N_SUB, d)
    cos_t, sin_t = _rope_tables(positions)
    x2 = x.reshape(batch * seq, d)

    for l in range(depth):
        lambda_init = 0.8 - 0.6 * math.exp(-0.3 * l)
        x2 = _ffn_sublayer(x2, ada[l], norm_g[l], ffn1_w_in[l].astype(_BF16), ffn1_w_out[l].astype(_BF16),
                           j=0, res_w=0.5, seq=seq)
        q, k, v, xr, yr = _mix_in(x2, ada[l], norm_g[l], w_mix_in[l].astype(_BF16), cos_t, sin_t, seq=seq)
        attn = _attention(q, k, v, lambda_q[l], lambda_k[l], subln_g[l],
                          lambda_init=lambda_init, batch=batch, seq=seq)
        rec = _recurrent_group(xr, yr, conv_w[l], conv_b[l],
                               (0.5 * _block_diag_gates(lru_w_gate[l])).astype(_BF16),
                               lru_b_gate[l], lru_lambda[l], batch=batch, seq=seq)
        x2 = _mix_out(x2, attn, rec, ada[l], norm_g[l], w_mix_out[l].astype(_BF16), seq=seq)
        x2 = _ffn_sublayer(x2, ada[l], norm_g[l], ffn2_w_in[l].astype(_BF16), ffn2_w_out[l].astype(_BF16),
                           j=2, res_w=0.5, seq=seq)
    return x2.reshape(batch, seq, d)
```

```python
import functools
import math

import jax
import jax.numpy as jnp
from jax import lax
from jax.experimental import pallas as pl
from jax.experimental.pallas import tpu as pltpu

D_MODEL = 1024
D_ATTN = 512
D_LRU = 512
HEAD_DIM = 64
V_DIM = 128
N_ATTN_HEADS = 4
N_LRU_BLOCKS = 8
LRU_BLOCK = 64
CONV_WIDTH = 4
LRU_C = 8.0
MIX_IN = 3 * D_ATTN + 2 * D_LRU
D_FF = 2816
ROPE_THETA = 10000.0
EPS = 1e-6
N_SUB = 3
LOG2_E = 1.4426950408889634

LANES = 128
SUBLANES = 8
MXU_DIM = 256
VMEM_LIMIT_BYTES = 56 * 1024 * 1024

TOKEN_TILE = 1024
Q_TILE = 512
FF_CHUNK = MXU_DIM
ADA_COLS = 2304

SCAN_SEG = 64
SCAN_NSEG = 32
SCAN_PITCH = 72
LANE_GROUPS = D_LRU // LANES

_F32 = jnp.float32
_BF16 = jnp.bfloat16


def _params(*sem):
    return pltpu.CompilerParams(dimension_semantics=sem, vmem_limit_bytes=VMEM_LIMIT_BYTES)


def _rms(x):
    return lax.rsqrt(jnp.mean(x * x, axis=-1, keepdims=True) + EPS)


def _ada_rows(ada_ref, j):
    shift = ada_ref[0, 3 * j:3 * j + 1, :]
    scale = ada_ref[0, 3 * j + 1:3 * j + 2, :]
    gate = ada_ref[0, 3 * j + 2:3 * j + 3, :]
    return shift, scale, gate


def _prenorm(x, g_pre, scale, shift):
    return (x * _rms(x)) * (g_pre * (1.0 + scale)) + shift


def _ada_kernel(c_ref, w_ref, b_ref, o_ref):
    c = c_ref[...]
    cond = (c * jax.nn.sigmoid(c)).astype(_BF16)
    w = w_ref[0].astype(_BF16)
    o_ref[0] = jnp.dot(cond, w, preferred_element_type=_F32) + b_ref[0]


def _ada_all_layers(c, w_ada, b_ada):
    depth, d, n = w_ada.shape
    b = c.shape[0]
    return pl.pallas_call(
        _ada_kernel,
        grid=(depth, n // ADA_COLS),
        in_specs=[
            pl.BlockSpec((b, d), lambda l, i: (0, 0)),
            pl.BlockSpec((1, d, ADA_COLS), lambda l, i: (l, 0, i)),
            pl.BlockSpec((1, 1, ADA_COLS), lambda l, i: (l, 0, i)),
        ],
        out_specs=pl.BlockSpec((1, b, ADA_COLS), lambda l, i: (l, 0, i)),
        out_shape=jax.ShapeDtypeStruct((depth, b, n), _F32),
        compiler_params=_params("parallel", "parallel"),
    )(c, w_ada, b_ada.reshape(depth, 1, n))


def _rope_kernel(pos_ref, inv_ref, sign_ref, cos_ref, sin_ref):
    ang = pos_ref[...].astype(_F32) * inv_ref[...]
    cos_ref[...] = jnp.cos(ang)
    sin_ref[...] = jnp.sin(ang) * sign_ref[...]


def _rope_tables(positions):
    n = positions.size
    inv = ROPE_THETA ** (-jnp.arange(0, HEAD_DIM, 2, dtype=_F32) / HEAD_DIM)
    reps = LANES // (HEAD_DIM // 2)
    inv_row = jnp.tile(inv, reps).reshape(1, LANES)
    half = jnp.concatenate([-jnp.ones((HEAD_DIM // 2,), _F32), jnp.ones((HEAD_DIM // 2,), _F32)])
    sign_row = jnp.tile(half, LANES // HEAD_DIM).reshape(1, LANES)
    row = pl.BlockSpec((1, LANES), lambda i: (0, 0))
    tab = pl.BlockSpec((TOKEN_TILE, LANES), lambda i: (i, 0))
    return pl.pallas_call(
        _rope_kernel,
        grid=(n // TOKEN_TILE,),
        in_specs=[pl.BlockSpec((TOKEN_TILE, 1), lambda i: (i, 0)), row, row],
        out_specs=[tab, tab],
        out_shape=[jax.ShapeDtypeStruct((n, LANES), _F32)] * 2,
        compiler_params=_params("parallel"),
    )(positions.reshape(n, 1), inv_row, sign_row)


def _ffn_kernel(*refs, j, res_w, after_mixer):
    if after_mixer:
        x_ref, attn_ref, rec_ref, wmix_ref, ada_ref, g_ref, win_ref, wout_ref, o_ref, act_ref = refs
        _, _, mix_gate = _ada_rows(ada_ref, 1)
        m = (jnp.dot(attn_ref[...], wmix_ref[:D_ATTN, :], preferred_element_type=_F32)
             + jnp.dot(rec_ref[...], wmix_ref[D_ATTN:, :], preferred_element_type=_F32))
        x = x_ref[...] + mix_gate * ((m * _rms(m)) * g_ref[3:4, :])
    else:
        x_ref, ada_ref, g_ref, win_ref, wout_ref, o_ref, act_ref = refs
        x = x_ref[...]
    shift, scale, gate = _ada_rows(ada_ref, j)
    h = _prenorm(x, g_ref[2 * j:2 * j + 1, :], scale, shift).astype(_BF16)
    for c in range(D_FF // FF_CHUNK):
        lo = c * FF_CHUNK
        g = jnp.dot(h, win_ref[:, lo:lo + FF_CHUNK], preferred_element_type=_F32)
        u = jnp.dot(h, win_ref[:, D_FF + lo:D_FF + lo + FF_CHUNK], preferred_element_type=_F32)
        act_ref[:, lo:lo + FF_CHUNK] = (g * jax.nn.sigmoid(g) * u).astype(_BF16)
    y = jnp.dot(act_ref[...], wout_ref[...], preferred_element_type=_F32)
    y = (y * _rms(y)) * g_ref[2 * j + 1:2 * j + 2, :]
    o_ref[...] = x + (res_w * gate) * y


def _layer_block(shape, l, **kw):
    zeros = (0,) * len(shape)
    return pl.BlockSpec((None,) + tuple(shape), lambda i: (l,) + zeros, **kw)


def _ada_block(l, tiles_per_seq):
    return pl.BlockSpec((None, 1, 3 * N_SUB, D_MODEL), lambda i: (l, i // tiles_per_seq, 0, 0))


def _ffn_sublayer(x2, ada, norm_g, w_in, w_out, *, l, j, res_w, seq, mixer=None):
    n, d = x2.shape
    tiles_per_seq = seq // TOKEN_TILE
    resident = dict(pipeline_mode=pl.Buffered(1))
    tok = pl.BlockSpec((TOKEN_TILE, d), lambda i: (i, 0))
    in_specs, args = [tok], [x2]
    if mixer is not None:
        col = pl.BlockSpec((TOKEN_TILE, D_ATTN), lambda i: (i, 0))
        in_specs += [col, col, _layer_block((d, d), l, **resident)]
        args += list(mixer)
    in_specs += [
        _ada_block(l, tiles_per_seq),
        _layer_block((2 * N_SUB, d), l),
        _layer_block((d, 2 * D_FF), l, **resident),
        _layer_block((D_FF, d), l, **resident),
    ]
    args += [ada, norm_g, w_in, w_out]
    return pl.pallas_call(
        functools.partial(_ffn_kernel, j=j, res_w=res_w, after_mixer=mixer is not None),
        grid=(n // TOKEN_TILE,),
        in_specs=in_specs,
        out_specs=tok,
        out_shape=jax.ShapeDtypeStruct((n, d), _F32),
        scratch_shapes=[pltpu.VMEM((TOKEN_TILE, D_FF), _BF16)],
        compiler_params=_params("parallel"),
    )(*args)


def _rope(t, cos, sin_signed):
    lane = lax.broadcasted_iota(jnp.int32, t.shape, 1)
    first_half = (lane % HEAD_DIM) < (HEAD_DIM // 2)
    partner = jnp.where(first_half,
                        pltpu.roll(t, LANES - HEAD_DIM // 2, axis=1),
                        pltpu.roll(t, HEAD_DIM // 2, axis=1))
    return t * cos + partner * sin_signed


def _mix_in_kernel(x_ref, ada_ref, g_ref, w_ref, cos_ref, sin_ref,
                   q_ref, k_ref, v_ref, xr_ref, yr_ref):
    j = 1
    x = x_ref[...]
    shift, scale, _ = _ada_rows(ada_ref, j)
    h = _prenorm(x, g_ref[2 * j:2 * j + 1, :], scale, shift).astype(_BF16)
    cos = cos_ref[...]
    sin = sin_ref[...]
    q_scale = HEAD_DIM ** -0.5 * LOG2_E
    q = jnp.dot(h, w_ref[:, :D_ATTN], preferred_element_type=_F32)
    k = jnp.dot(h, w_ref[:, D_ATTN:2 * D_ATTN], preferred_element_type=_F32)
    for hd in range(N_ATTN_HEADS):
        head = slice(hd * LANES, (hd + 1) * LANES)
        q_ref[:, head] = (_rope(q[:, head], cos, sin) * q_scale).astype(_BF16)
        k_ref[:, head] = _rope(k[:, head], cos, sin).astype(_BF16)
    v_ref[...] = jnp.dot(h, w_ref[:, 2 * D_ATTN:3 * D_ATTN], preferred_element_type=_F32).astype(_BF16)
    xr_ref[...] = jnp.dot(h, w_ref[:, 3 * D_ATTN:3 * D_ATTN + D_LRU], preferred_element_type=_F32)
    yr_ref[...] = jnp.dot(h, w_ref[:, 3 * D_ATTN + D_LRU:], preferred_element_type=_F32)


def _mix_in(x2, ada, norm_g, w_in, cos_t, sin_t, *, l, seq):
    n, d = x2.shape
    tiles_per_seq = seq // TOKEN_TILE
    half = lambda dt: jax.ShapeDtypeStruct((n, D_ATTN), dt)
    col = pl.BlockSpec((TOKEN_TILE, D_ATTN), lambda i: (i, 0))
    tab = pl.BlockSpec((TOKEN_TILE, LANES), lambda i: (i, 0))
    return pl.pallas_call(
        _mix_in_kernel,
        grid=(n // TOKEN_TILE,),
        in_specs=[
            pl.BlockSpec((TOKEN_TILE, d), lambda i: (i, 0)),
            _ada_block(l, tiles_per_seq),
            _layer_block((2 * N_SUB, d), l),
            _layer_block((d, MIX_IN), l),
            tab, tab,
        ],
        out_specs=[col] * 5,
        out_shape=[half(_BF16), half(_BF16), half(_BF16), half(_F32), half(_F32)],
        compiler_params=_params("parallel"),
    )(x2, ada, norm_g, w_in, cos_t, sin_t)


def _store_scores(q, k, s_ref):
    lane = lax.broadcasted_iota(jnp.int32, q.shape, 1)
    first_map = lane < HEAD_DIM
    zero = jnp.zeros_like(q)
    nt = (((1,), (1,)), ((), ()))
    s_ref[0] = lax.dot_general(jnp.where(first_map, q, zero), k, nt, preferred_element_type=_F32)
    s_ref[1] = lax.dot_general(jnp.where(first_map, zero, q), k, nt, preferred_element_type=_F32)


def _diff_softmax_pv(s_ref, v_ones, lam, out_gain):
    maps = []
    for m in range(2):
        s = s_ref[m]
        p = jnp.exp2(s - jnp.max(s, axis=-1, keepdims=True))
        pv = jnp.dot(p.astype(_BF16), v_ones, preferred_element_type=_F32)
        maps.append(pv[:, :V_DIM] * (1.0 / pv[:, V_DIM:]))
    o = maps[0] - lam * maps[1]
    return (o * _rms(o)) * out_gain


def _attn_kernel(q0_ref, q1_ref, q2_ref, k_ref, kn_ref, v_ref, lq_ref, lk_ref, sg_ref, o_ref,
                 sa_ref, sb_ref, *, lambda_init):
    @pl.when(pl.program_id(0) == 0)
    def _():
        _store_scores(q0_ref[...], k_ref[...], sa_ref)

    v_ones = jnp.concatenate([v_ref[...], jnp.ones(v_ref.shape, _BF16)], axis=1)
    lqk = lq_ref[...] * lk_ref[...]
    lam = (jnp.exp(jnp.sum(lqk[0:1, :], axis=-1, keepdims=True))
           - jnp.exp(jnp.sum(lqk[1:2, :], axis=-1, keepdims=True)) + lambda_init)
    out_gain = sg_ref[...] * (1.0 - lambda_init)

    _store_scores(q1_ref[...], k_ref[...], sb_ref)
    o_ref[:Q_TILE, :] = _diff_softmax_pv(sa_ref, v_ones, lam, out_gain).astype(o_ref.dtype)
    _store_scores(q2_ref[...], kn_ref[...], sa_ref)
    o_ref[Q_TILE:, :] = _diff_softmax_pv(sb_ref, v_ones, lam, out_gain).astype(o_ref.dtype)


def _attention(q, k, v, lq, lk, sg, *, lambda_init, batch, seq):
    n = q.shape[0]
    q_tiles = seq // Q_TILE
    n_tiles = batch * N_ATTN_HEADS * q_tiles
    assert q_tiles % 2 == 0

    def q_block(tile):
        bh, i = tile // q_tiles, tile % q_tiles
        return (bh // N_ATTN_HEADS) * q_tiles + i, bh % N_ATTN_HEADS

    def kv_block(tile):
        bh = tile // q_tiles
        return bh // N_ATTN_HEADS, bh % N_ATTN_HEADS

    def out_block(step):
        row, col = q_block(2 * step)
        return row // 2, col

    last = n_tiles - 1
    q_spec = lambda tile_of: pl.BlockSpec((Q_TILE, V_DIM), lambda t: q_block(tile_of(t)))
    kv_spec = lambda tile_of: pl.BlockSpec((seq, V_DIM), lambda t: kv_block(tile_of(t)))
    small = lambda shape: pl.BlockSpec(shape, lambda t: (0, 0))
    scores = pltpu.VMEM((2, Q_TILE, seq), _F32)
    return pl.pallas_call(
        functools.partial(_attn_kernel, lambda_init=lambda_init),
        grid=(n_tiles // 2,),
        in_specs=[
            q_spec(lambda t: 2 * t),
            q_spec(lambda t: 2 * t + 1),
            q_spec(lambda t: jnp.minimum(2 * t + 2, last)),
            kv_spec(lambda t: 2 * t),
            kv_spec(lambda t: jnp.minimum(2 * t + 2, last)),
            kv_spec(lambda t: 2 * t),
            small((2, HEAD_DIM)), small((2, HEAD_DIM)), small((1, V_DIM)),
        ],
        out_specs=pl.BlockSpec((2 * Q_TILE, V_DIM), lambda t: out_block(t)),
        out_shape=jax.ShapeDtypeStruct((n, D_ATTN), _BF16),
        scratch_shapes=[scores, scores],
        compiler_params=_params("arbitrary"),
    )(q, q, q, k, k, v, lq, lk, sg.reshape(1, V_DIM))


def _scan_direction(a_scr, b_scr, p_scr, h_scr, *, reverse):
    chains = [(lg, sg) for lg in range(LANE_GROUPS) for sg in range(SCAN_NSEG // SUBLANES)]

    def seg_rows(sg, j):
        return pl.ds(sg * SUBLANES * SCAN_PITCH + j, SUBLANES, stride=SCAN_PITCH)

    def step(jj, carry):
        j = (SCAN_SEG - 1 - jj) if reverse else jj
        hs, ps = carry
        new_h, new_p = [], []
        for c, (lg, sg) in enumerate(chains):
            a = a_scr[lg, seg_rows(sg, j), :]
            b = b_scr[lg, seg_rows(sg, j), :]
            h = a * hs[c] + b
            p = a * ps[c]
            h_scr[lg, seg_rows(sg, j), :] = h
            p_scr[lg, seg_rows(sg, j), :] = p
            new_h.append(h)
            new_p.append(p)
        return tuple(new_h), tuple(new_p)

    zeros = tuple(jnp.zeros((SUBLANES, LANES), _F32) for _ in chains)
    ones = tuple(jnp.ones((SUBLANES, LANES), _F32) for _ in chains)
    h_end, p_tot = lax.fori_loop(0, SCAN_SEG, step, (zeros, ones), unroll=8)

    order = range(SCAN_NSEG - 1, -1, -1) if reverse else range(SCAN_NSEG)
    carries = []
    for lg in range(LANE_GROUPS):
        carry_in = jnp.zeros((1, LANES), _F32)
        per_seg = [None] * SCAN_NSEG
        for s in order:
            c = chains.index((lg, s // SUBLANES))
            r = s % SUBLANES
            per_seg[s] = carry_in
            carry_in = p_tot[c][r:r + 1, :] * carry_in + h_end[c][r:r + 1, :]
        carries.append(per_seg)
    return carries


def _depthwise_conv(x_ref, cw, cb):
    seq = x_ref.shape[0]

    def taps(x, m2, m1, p1):
        return cb + cw[2:3, :] * x + cw[0:1, :] * m2 + cw[1:2, :] * m1 + cw[3:4, :] * p1

    row = lax.broadcasted_iota(jnp.int32, (SUBLANES, x_ref.shape[1]), 0)
    lo, hi = SUBLANES, seq - SUBLANES
    x_top = x_ref[0:lo, :]
    x_bot = x_ref[hi:seq, :]
    return jnp.concatenate([
        taps(x_top,
             jnp.where(row >= 2, pltpu.roll(x_top, 2, axis=0), 0.0),
             jnp.where(row >= 1, pltpu.roll(x_top, 1, axis=0), 0.0),
             x_ref[1:lo + 1, :]),
        taps(x_ref[lo:hi, :], x_ref[lo - 2:hi - 2, :], x_ref[lo - 1:hi - 1, :], x_ref[lo + 1:hi + 1, :]),
        taps(x_bot, x_ref[hi - 2:seq - 2, :], x_ref[hi - 1:seq - 1, :],
             jnp.where(row < SUBLANES - 1, pltpu.roll(x_bot, SUBLANES - 1, axis=0), 0.0)),
    ], axis=0)


def _lru_kernel(xr_ref, yr_ref, cw_ref, cb_ref, wg_ref, bg_ref, lam_ref, o_ref,
                a_scr, b_scr, p_scr, h_scr, sum_scr):
    xc = _depthwise_conv(xr_ref, cw_ref[...], cb_ref[...])
    xcb = xc.astype(_BF16)
    xc_half = 0.5 * xc

    for d in range(2):
        lam = lam_ref[d:d + 1, :]
        half_c = (-0.5 * LRU_C) * (jnp.maximum(-lam, 0.0) + jnp.log1p(jnp.exp(-jnp.abs(lam))))
        for half in range(D_LRU // MXU_DIM):
            cols = slice(half * MXU_DIM, (half + 1) * MXU_DIM)
            xh = xcb[:, cols]
            gr = jnp.dot(xh, wg_ref[d, 0, half], preferred_element_type=_F32)
            gi = jnp.dot(xh, wg_ref[d, 1, half], preferred_element_type=_F32)
            t_r = jnp.tanh(gr + 0.5 * bg_ref[d, 0:1, cols])
            t_i = jnp.tanh(gi + 0.5 * bg_ref[d, 1:2, cols])
            log_a = half_c[:, cols] * t_r + half_c[:, cols]
            a = jnp.exp(log_a)
            one_minus_a2 = jnp.tanh(log_a) * (-1.0 - a * a)
            mult = jnp.where(one_minus_a2 > 0.0, one_minus_a2 * lax.rsqrt(one_minus_a2), 0.0)
            b = mult * ((t_i + 1.0) * xc_half[:, cols])
            for sub in range(MXU_DIM // LANES):
                lg = half * (MXU_DIM // LANES) + sub
                lanes = slice(sub * LANES, (sub + 1) * LANES)
                for s in range(SCAN_NSEG):
                    src = slice(s * SCAN_SEG, (s + 1) * SCAN_SEG)
                    dst = pl.ds(s * SCAN_PITCH, SCAN_SEG)
                    a_scr[lg, dst, :] = a[src, lanes]
                    b_scr[lg, dst, :] = b[src, lanes]
        carries = _scan_direction(a_scr, b_scr, p_scr, h_scr, reverse=(d == 1))
        for lg in range(LANE_GROUPS):
            for s in range(SCAN_NSEG):
                rows = pl.ds(s * SCAN_PITCH, SCAN_SEG)
                dst = (slice(s * SCAN_SEG, (s + 1) * SCAN_SEG), slice(lg * LANES, (lg + 1) * LANES))
                hseg = h_scr[lg, rows, :] + p_scr[lg, rows, :] * carries[lg][s]
                if d == 0:
                    sum_scr[dst] = hseg
                else:
                    sum_scr[dst] = sum_scr[dst] + hseg
    o_ref[...] = (sum_scr[...] * jax.nn.gelu(yr_ref[...])).astype(o_ref.dtype)


def _recurrent_group(xr, yr, conv_w, conv_b, wg_bd, b_gate, lam, *, batch, seq):
    n = xr.shape[0]
    tok = pl.BlockSpec((seq, D_LRU), lambda b: (b, 0))
    scan_buf = pltpu.VMEM((LANE_GROUPS, SCAN_NSEG * SCAN_PITCH, LANES), _F32)
    return pl.pallas_call(
        _lru_kernel,
        grid=(batch,),
        in_specs=[
            tok, tok,
            pl.BlockSpec((CONV_WIDTH, D_LRU), lambda b: (0, 0)),
            pl.BlockSpec((1, D_LRU), lambda b: (0, 0)),
            pl.BlockSpec(wg_bd.shape, lambda b: (0, 0, 0, 0, 0)),
            pl.BlockSpec((2, 2, D_LRU), lambda b: (0, 0, 0)),
            pl.BlockSpec((2, D_LRU), lambda b: (0, 0)),
        ],
        out_specs=tok,
        out_shape=jax.ShapeDtypeStruct((n, D_LRU), _BF16),
        scratch_shapes=[scan_buf] * 4 + [pltpu.VMEM((seq, D_LRU), _F32)],
        compiler_params=_params("parallel"),
    )(xr, yr, conv_w, conv_b.reshape(1, D_LRU), wg_bd, b_gate, lam)


def _block_diag_gates(w_gate):
    per_tile = MXU_DIM // LRU_BLOCK
    w = w_gate.reshape(2, 2, N_LRU_BLOCKS // per_tile, per_tile, LRU_BLOCK, LRU_BLOCK)
    eye = jnp.eye(per_tile, dtype=w.dtype)
    bd = jnp.einsum('dghnij,nm->dghnimj', w, eye)
    return bd.reshape(2, 2, N_LRU_BLOCKS // per_tile, MXU_DIM, MXU_DIM)


def kernel(x, c, positions, w_ada, b_ada, norm_g, ffn1_w_in, ffn1_w_out, ffn2_w_in, ffn2_w_out,
           w_mix_in, w_mix_out, lambda_q, lambda_k, subln_g, conv_w, conv_b,
           lru_w_gate, lru_b_gate, lru_lambda):
    batch, seq, d = x.shape
    depth = w_ada.shape[0]
    assert (d, seq % TOKEN_TILE, seq % Q_TILE) == (D_MODEL, 0, 0)
    assert seq == SCAN_SEG * SCAN_NSEG

    ada = _ada_all_layers(c, w_ada, b_ada).reshape(depth, batch, 3 * N_SUB, d)
    cos_t, sin_t = _rope_tables(positions)
    x2 = x.reshape(batch * seq, d)
    ffn1_in, ffn1_out, ffn2_in, ffn2_out, mix_in_w, mix_out_w = (
        w.astype(_BF16) for w in (ffn1_w_in, ffn1_w_out, ffn2_w_in, ffn2_w_out, w_mix_in, w_mix_out))

    for l in range(depth):
        lambda_init = 0.8 - 0.6 * math.exp(-0.3 * l)
        x2 = _ffn_sublayer(x2, ada, norm_g, ffn1_in, ffn1_out, l=l, j=0, res_w=0.5, seq=seq)
        q, k, v, xr, yr = _mix_in(x2, ada, norm_g, mix_in_w, cos_t, sin_t, l=l, seq=seq)
        attn = _attention(q, k, v, lambda_q[l], lambda_k[l], subln_g[l],
                          lambda_init=lambda_init, batch=batch, seq=seq)
        rec = _recurrent_group(xr, yr, conv_w[l], conv_b[l],
                               (0.5 * _block_diag_gates(lru_w_gate[l])).astype(_BF16),
                               lru_b_gate[l], lru_lambda[l], batch=batch, seq=seq)
        x2 = _ffn_sublayer(x2, ada, norm_g, ffn2_in, ffn2_out, l=l, j=2, res_w=0.5, seq=seq,
                           mixer=(attn, rec, mix_out_w))
    return x2.reshape(batch, seq, d)
```

```python
import functools
import math

import jax
import jax.numpy as jnp
from jax import lax
from jax.experimental import pallas as pl
from jax.experimental.pallas import tpu as pltpu

D_MODEL = 1024
D_ATTN = 512
D_LRU = 512
HEAD_DIM = 64
V_DIM = 128
N_ATTN_HEADS = 4
N_LRU_BLOCKS = 8
LRU_BLOCK = 64
CONV_WIDTH = 4
LRU_C = 8.0
MIX_IN = 3 * D_ATTN + 2 * D_LRU
D_FF = 2816
ROPE_THETA = 10000.0
EPS = 1e-6
N_SUB = 3
LOG2_E = 1.4426950408889634

LANES = 128
SUBLANES = 8
MXU_DIM = 256
VMEM_LIMIT_BYTES = 56 * 1024 * 1024

TOKEN_TILE = 1024
Q_TILE = 512
FF_CHUNK = MXU_DIM
ADA_COLS = 2304

SCAN_SEG = 64
SCAN_NSEG = 32
SCAN_PITCH = 68
LANE_GROUPS = D_LRU // LANES

_F32 = jnp.float32
_BF16 = jnp.bfloat16


def _params(*sem):
    return pltpu.CompilerParams(dimension_semantics=sem, vmem_limit_bytes=VMEM_LIMIT_BYTES)


def _rms(x):
    return lax.rsqrt(jnp.mean(x * x, axis=-1, keepdims=True) + EPS)


def _ada_rows(ada_ref, j):
    shift = ada_ref[0, 3 * j:3 * j + 1, :]
    scale = ada_ref[0, 3 * j + 1:3 * j + 2, :]
    gate = ada_ref[0, 3 * j + 2:3 * j + 3, :]
    return shift, scale, gate


def _prenorm(x, g_pre, scale, shift):
    return (x * _rms(x)) * (g_pre * (1.0 + scale)) + shift


def _ada_kernel(c_ref, w_ref, b_ref, o_ref):
    c = c_ref[...]
    cond = (c * jax.nn.sigmoid(c)).astype(_BF16)
    w = w_ref[0].astype(_BF16)
    o_ref[0] = jnp.dot(cond, w, preferred_element_type=_F32) + b_ref[0]


def _ada_all_layers(c, w_ada, b_ada):
    depth, d, n = w_ada.shape
    b = c.shape[0]
    return pl.pallas_call(
        _ada_kernel,
        grid=(depth, n // ADA_COLS),
        in_specs=[
            pl.BlockSpec((b, d), lambda l, i: (0, 0)),
            pl.BlockSpec((1, d, ADA_COLS), lambda l, i: (l, 0, i)),
            pl.BlockSpec((1, 1, ADA_COLS), lambda l, i: (l, 0, i)),
        ],
        out_specs=pl.BlockSpec((1, b, ADA_COLS), lambda l, i: (l, 0, i)),
        out_shape=jax.ShapeDtypeStruct((depth, b, n), _F32),
        compiler_params=_params("parallel", "parallel"),
    )(c, w_ada, b_ada.reshape(depth, 1, n))


def _rope_kernel(pos_ref, inv_ref, sign_ref, cos_ref, sin_ref):
    ang = pos_ref[...].astype(_F32) * inv_ref[...]
    cos_ref[...] = jnp.cos(ang)
    sin_ref[...] = jnp.sin(ang) * sign_ref[...]


def _rope_tables(positions):
    n = positions.size
    inv = ROPE_THETA ** (-jnp.arange(0, HEAD_DIM, 2, dtype=_F32) / HEAD_DIM)
    reps = LANES // (HEAD_DIM // 2)
    inv_row = jnp.tile(inv, reps).reshape(1, LANES)
    half = jnp.concatenate([-jnp.ones((HEAD_DIM // 2,), _F32), jnp.ones((HEAD_DIM // 2,), _F32)])
    sign_row = jnp.tile(half, LANES // HEAD_DIM).reshape(1, LANES)
    row = pl.BlockSpec((1, LANES), lambda i: (0, 0))
    tab = pl.BlockSpec((TOKEN_TILE, LANES), lambda i: (i, 0))
    return pl.pallas_call(
        _rope_kernel,
        grid=(n // TOKEN_TILE,),
        in_specs=[pl.BlockSpec((TOKEN_TILE, 1), lambda i: (i, 0)), row, row],
        out_specs=[tab, tab],
        out_shape=[jax.ShapeDtypeStruct((n, LANES), _F32)] * 2,
        compiler_params=_params("parallel"),
    )(positions.reshape(n, 1), inv_row, sign_row)


def _ffn_kernel(*refs, j, res_w, after_mixer):
    if after_mixer:
        x_ref, attn_ref, rec_ref, wmix_ref, ada_ref, g_ref, win_ref, wout_ref, o_ref, act_ref = refs
        _, _, mix_gate = _ada_rows(ada_ref, 1)
        m = (jnp.dot(attn_ref[...], wmix_ref[:D_ATTN, :], preferred_element_type=_F32)
             + jnp.dot(rec_ref[...], wmix_ref[D_ATTN:, :], preferred_element_type=_F32))
        x = x_ref[...] + mix_gate * ((m * _rms(m)) * g_ref[3:4, :])
    else:
        x_ref, ada_ref, g_ref, win_ref, wout_ref, o_ref, act_ref = refs
        x = x_ref[...]
    shift, scale, gate = _ada_rows(ada_ref, j)
    h = _prenorm(x, g_ref[2 * j:2 * j + 1, :], scale, shift).astype(_BF16)
    for c in range(D_FF // FF_CHUNK):
        lo = c * FF_CHUNK
        g = jnp.dot(h, win_ref[:, lo:lo + FF_CHUNK], preferred_element_type=_F32)
        u = jnp.dot(h, win_ref[:, D_FF + lo:D_FF + lo + FF_CHUNK], preferred_element_type=_F32)
        act_ref[:, lo:lo + FF_CHUNK] = (g * jax.nn.sigmoid(g) * u).astype(_BF16)
    y = jnp.dot(act_ref[...], wout_ref[...], preferred_element_type=_F32)
    y = (y * _rms(y)) * g_ref[2 * j + 1:2 * j + 2, :]
    o_ref[...] = x + (res_w * gate) * y


def _layer_block(shape, l, **kw):
    zeros = (0,) * len(shape)
    return pl.BlockSpec((None,) + tuple(shape), lambda i: (l,) + zeros, **kw)


def _ada_block(l, tiles_per_seq):
    return pl.BlockSpec((None, 1, 3 * N_SUB, D_MODEL), lambda i: (l, i // tiles_per_seq, 0, 0))


def _ffn_sublayer(x2, ada, norm_g, w_in, w_out, *, l, j, res_w, seq, mixer=None):
    n, d = x2.shape
    tiles_per_seq = seq // TOKEN_TILE
    resident = dict(pipeline_mode=pl.Buffered(1))
    tok = pl.BlockSpec((TOKEN_TILE, d), lambda i: (i, 0))
    in_specs, args = [tok], [x2]
    if mixer is not None:
        col = pl.BlockSpec((TOKEN_TILE, D_ATTN), lambda i: (i, 0))
        in_specs += [col, col, _layer_block((d, d), l, **resident)]
        args += list(mixer)
    in_specs += [
        _ada_block(l, tiles_per_seq),
        _layer_block((2 * N_SUB, d), l),
        _layer_block((d, 2 * D_FF), l, **resident),
        _layer_block((D_FF, d), l, **resident),
    ]
    args += [ada, norm_g, w_in, w_out]
    return pl.pallas_call(
        functools.partial(_ffn_kernel, j=j, res_w=res_w, after_mixer=mixer is not None),
        grid=(n // TOKEN_TILE,),
        in_specs=in_specs,
        out_specs=tok,
        out_shape=jax.ShapeDtypeStruct((n, d), _F32),
        scratch_shapes=[pltpu.VMEM((TOKEN_TILE, D_FF), _BF16)],
        compiler_params=_params("parallel"),
    )(*args)


def _rope(t, cos, sin_signed):
    lane = lax.broadcasted_iota(jnp.int32, t.shape, 1)
    first_half = (lane % HEAD_DIM) < (HEAD_DIM // 2)
    partner = jnp.where(first_half,
                        pltpu.roll(t, LANES - HEAD_DIM // 2, axis=1),
                        pltpu.roll(t, HEAD_DIM // 2, axis=1))
    return t * cos + partner * sin_signed


def _mix_in_kernel(x_ref, ada_ref, g_ref, w_ref, cos_ref, sin_ref,
                   q_ref, k_ref, v_ref, xr_ref, gy_ref):
    j = 1
    x = x_ref[...]
    shift, scale, _ = _ada_rows(ada_ref, j)
    h = _prenorm(x, g_ref[2 * j:2 * j + 1, :], scale, shift).astype(_BF16)
    cos = cos_ref[...]
    sin = sin_ref[...]
    q_scale = HEAD_DIM ** -0.5 * LOG2_E
    q = jnp.dot(h, w_ref[:, :D_ATTN], preferred_element_type=_F32)
    k = jnp.dot(h, w_ref[:, D_ATTN:2 * D_ATTN], preferred_element_type=_F32)
    for hd in range(N_ATTN_HEADS):
        head = slice(hd * LANES, (hd + 1) * LANES)
        q_ref[:, head] = (_rope(q[:, head], cos, sin) * q_scale).astype(_BF16)
        k_ref[:, head] = _rope(k[:, head], cos, sin).astype(_BF16)
    v_ref[...] = jnp.dot(h, w_ref[:, 2 * D_ATTN:3 * D_ATTN], preferred_element_type=_F32).astype(_BF16)
    xr_ref[...] = jnp.dot(h, w_ref[:, 3 * D_ATTN:3 * D_ATTN + D_LRU], preferred_element_type=_F32)
    gy_ref[...] = jax.nn.gelu(jnp.dot(h, w_ref[:, 3 * D_ATTN + D_LRU:], preferred_element_type=_F32))


def _mix_in(x2, ada, norm_g, w_in, cos_t, sin_t, *, l, seq):
    n, d = x2.shape
    tiles_per_seq = seq // TOKEN_TILE
    half = lambda dt: jax.ShapeDtypeStruct((n, D_ATTN), dt)
    col = pl.BlockSpec((TOKEN_TILE, D_ATTN), lambda i: (i, 0))
    tab = pl.BlockSpec((TOKEN_TILE, LANES), lambda i: (i, 0))
    return pl.pallas_call(
        _mix_in_kernel,
        grid=(n // TOKEN_TILE,),
        in_specs=[
            pl.BlockSpec((TOKEN_TILE, d), lambda i: (i, 0)),
            _ada_block(l, tiles_per_seq),
            _layer_block((2 * N_SUB, d), l),
            _layer_block((d, MIX_IN), l),
            tab, tab,
        ],
        out_specs=[col] * 5,
        out_shape=[half(_BF16), half(_BF16), half(_BF16), half(_F32), half(_F32)],
        compiler_params=_params("parallel"),
    )(x2, ada, norm_g, w_in, cos_t, sin_t)


def _store_scores(q, k, s_ref):
    lane = lax.broadcasted_iota(jnp.int32, q.shape, 1)
    first_map = lane < HEAD_DIM
    zero = jnp.zeros_like(q)
    nt = (((1,), (1,)), ((), ()))
    s_ref[0] = lax.dot_general(jnp.where(first_map, q, zero), k, nt, preferred_element_type=_F32)
    s_ref[1] = lax.dot_general(jnp.where(first_map, zero, q), k, nt, preferred_element_type=_F32)


def _diff_softmax_pv(s_ref, v_ones, lam, out_gain):
    maps = []
    for m in range(2):
        s = s_ref[m]
        p = jnp.exp2(s - jnp.max(s, axis=-1, keepdims=True))
        pv = jnp.dot(p.astype(_BF16), v_ones, preferred_element_type=_F32)
        maps.append(pv[:, :V_DIM] * (1.0 / pv[:, V_DIM:]))
    o = maps[0] - lam * maps[1]
    return (o * _rms(o)) * out_gain


def _attn_kernel(q_ref, qn_ref, k_ref, kn_ref, v_ref, lq_ref, lk_ref, sg_ref, o_ref,
                 s0_ref, s1_ref, *, lambda_init):
    q_tiles = q_ref.shape[0] // Q_TILE
    tile = lambda i: slice(i * Q_TILE, (i + 1) * Q_TILE)
    scores = (s0_ref, s1_ref)

    @pl.when(pl.program_id(0) == 0)
    def _():
        _store_scores(q_ref[tile(0), :], k_ref[...], scores[0])

    v_ones = jnp.concatenate([v_ref[...], jnp.ones(v_ref.shape, _BF16)], axis=1)
    lqk = lq_ref[...] * lk_ref[...]
    lam = (jnp.exp(jnp.sum(lqk[0:1, :], axis=-1, keepdims=True))
           - jnp.exp(jnp.sum(lqk[1:2, :], axis=-1, keepdims=True)) + lambda_init)
    out_gain = sg_ref[...] * (1.0 - lambda_init)

    for i in range(q_tiles):
        if i + 1 < q_tiles:
            _store_scores(q_ref[tile(i + 1), :], k_ref[...], scores[(i + 1) % 2])
        else:
            _store_scores(qn_ref[...], kn_ref[...], scores[(i + 1) % 2])
        o_ref[tile(i), :] = _diff_softmax_pv(scores[i % 2], v_ones, lam, out_gain).astype(o_ref.dtype)


def _attention(q, k, v, lq, lk, sg, *, lambda_init, batch, seq):
    n = q.shape[0]
    q_tiles = seq // Q_TILE
    assert q_tiles % 2 == 0
    heads = batch * N_ATTN_HEADS

    def head_block(t):
        return t // N_ATTN_HEADS, t % N_ATTN_HEADS

    def next_head(t):
        return jnp.minimum(t + 1, heads - 1)

    def next_first_tile(t):
        b, h = head_block(next_head(t))
        return b * q_tiles, h

    whole = pl.BlockSpec((seq, V_DIM), head_block)
    whole_next = pl.BlockSpec((seq, V_DIM), lambda t: head_block(next_head(t)))
    small = lambda shape: pl.BlockSpec(shape, lambda t: (0, 0))
    scores = pltpu.VMEM((2, Q_TILE, seq), _F32)
    return pl.pallas_call(
        functools.partial(_attn_kernel, lambda_init=lambda_init),
        grid=(heads,),
        in_specs=[
            whole,
            pl.BlockSpec((Q_TILE, V_DIM), next_first_tile),
            whole, whole_next,
            whole,
            small((2, HEAD_DIM)), small((2, HEAD_DIM)), small((1, V_DIM)),
        ],
        out_specs=whole,
        out_shape=jax.ShapeDtypeStruct((n, D_ATTN), _BF16),
        scratch_shapes=[scores, scores],
        compiler_params=_params("arbitrary"),
    )(q, q, k, k, v, lq, lk, sg.reshape(1, V_DIM))


def _scan_direction(a_scr, b_scr, p_scr, h_scr, *, reverse):
    chains = [(lg, sg) for lg in range(LANE_GROUPS) for sg in range(SCAN_NSEG // SUBLANES)]

    def seg_rows(sg, j):
        return pl.ds(sg * SUBLANES * SCAN_PITCH + j, SUBLANES, stride=SCAN_PITCH)

    def step(jj, carry):
        j = (SCAN_SEG - 1 - jj) if reverse else jj
        hs, ps = carry
        new_h, new_p = [], []
        for c, (lg, sg) in enumerate(chains):
            a = a_scr[lg, seg_rows(sg, j), :]
            b = b_scr[lg, seg_rows(sg, j), :]
            h = a * hs[c] + b
            p = a * ps[c]
            h_scr[lg, seg_rows(sg, j), :] = h
            p_scr[lg, seg_rows(sg, j), :] = p
            new_h.append(h)
            new_p.append(p)
        return tuple(new_h), tuple(new_p)

    zeros = tuple(jnp.zeros((SUBLANES, LANES), _F32) for _ in chains)
    ones = tuple(jnp.ones((SUBLANES, LANES), _F32) for _ in chains)
    h_end, p_tot = lax.fori_loop(0, SCAN_SEG, step, (zeros, ones), unroll=8)

    order = range(SCAN_NSEG - 1, -1, -1) if reverse else range(SCAN_NSEG)
    carries = []
    for lg in range(LANE_GROUPS):
        carry_in = jnp.zeros((1, LANES), _F32)
        per_seg = [None] * SCAN_NSEG
        for s in order:
            c = chains.index((lg, s // SUBLANES))
            r = s % SUBLANES
            per_seg[s] = carry_in
            carry_in = p_tot[c][r:r + 1, :] * carry_in + h_end[c][r:r + 1, :]
        carries.append(per_seg)
    return carries


def _depthwise_conv(x_ref, cw, cb):
    seq = x_ref.shape[0]

    def taps(x, m2, m1, p1):
        return cb + cw[2:3, :] * x + cw[0:1, :] * m2 + cw[1:2, :] * m1 + cw[3:4, :] * p1

    row = lax.broadcasted_iota(jnp.int32, (SUBLANES, x_ref.shape[1]), 0)
    lo, hi = SUBLANES, seq - SUBLANES
    x_top = x_ref[0:lo, :]
    x_bot = x_ref[hi:seq, :]
    return jnp.concatenate([
        taps(x_top,
             jnp.where(row >= 2, pltpu.roll(x_top, 2, axis=0), 0.0),
             jnp.where(row >= 1, pltpu.roll(x_top, 1, axis=0), 0.0),
             x_ref[1:lo + 1, :]),
        taps(x_ref[lo:hi, :], x_ref[lo - 2:hi - 2, :], x_ref[lo - 1:hi - 1, :], x_ref[lo + 1:hi + 1, :]),
        taps(x_bot, x_ref[hi - 2:seq - 2, :], x_ref[hi - 1:seq - 1, :],
             jnp.where(row < SUBLANES - 1, pltpu.roll(x_bot, SUBLANES - 1, axis=0), 0.0)),
    ], axis=0)


def _lru_kernel(xr_ref, gy_ref, cw_ref, cb_ref, wg_ref, bg_ref, lam_ref, o_ref,
                a_scr, b_scr, p_scr, h_scr, sum_scr):
    xc = _depthwise_conv(xr_ref, cw_ref[...], cb_ref[...])
    xcb = xc.astype(_BF16)
    xc_half = 0.5 * xc

    for d in range(2):
        lam = lam_ref[d:d + 1, :]
        half_c = (-0.5 * LRU_C) * (jnp.maximum(-lam, 0.0) + jnp.log1p(jnp.exp(-jnp.abs(lam))))
        for half in range(D_LRU // MXU_DIM):
            cols = slice(half * MXU_DIM, (half + 1) * MXU_DIM)
            xh = xcb[:, cols]
            gr = jnp.dot(xh, wg_ref[d, 0, half], preferred_element_type=_F32)
            gi = jnp.dot(xh, wg_ref[d, 1, half], preferred_element_type=_F32)
            t_r = jnp.tanh(gr + 0.5 * bg_ref[d, 0:1, cols])
            t_i = jnp.tanh(gi + 0.5 * bg_ref[d, 1:2, cols])
            log_a = half_c[:, cols] * t_r + half_c[:, cols]
            a = jnp.exp(log_a)
            one_minus_a2 = jnp.tanh(log_a) * (-1.0 - a * a)
            mult = jnp.where(one_minus_a2 > 0.0, one_minus_a2 * lax.rsqrt(one_minus_a2), 0.0)
            b = mult * ((t_i + 1.0) * xc_half[:, cols])
            for sub in range(MXU_DIM // LANES):
                lg = half * (MXU_DIM // LANES) + sub
                lanes = slice(sub * LANES, (sub + 1) * LANES)
                for s in range(SCAN_NSEG):
                    src = slice(s * SCAN_SEG, (s + 1) * SCAN_SEG)
                    dst = pl.ds(s * SCAN_PITCH, SCAN_SEG)
                    a_scr[lg, dst, :] = a[src, lanes]
                    b_scr[lg, dst, :] = b[src, lanes]
        carries = _scan_direction(a_scr, b_scr, p_scr, h_scr, reverse=(d == 1))
        for lg in range(LANE_GROUPS):
            for s in range(SCAN_NSEG):
                rows = pl.ds(s * SCAN_PITCH, SCAN_SEG)
                dst = (slice(s * SCAN_SEG, (s + 1) * SCAN_SEG), slice(lg * LANES, (lg + 1) * LANES))
                hseg = h_scr[lg, rows, :] + p_scr[lg, rows, :] * carries[lg][s]
                if d == 0:
                    sum_scr[dst] = hseg
                else:
                    o_ref[dst] = ((sum_scr[dst] + hseg) * gy_ref[dst]).astype(o_ref.dtype)


def _recurrent_group(xr, gy, conv_w, conv_b, wg_bd, b_gate, lam, *, batch, seq):
    n = xr.shape[0]
    tok = pl.BlockSpec((seq, D_LRU), lambda b: (b, 0))
    scan_buf = pltpu.VMEM((LANE_GROUPS, SCAN_NSEG * SCAN_PITCH, LANES), _F32)
    return pl.pallas_call(
        _lru_kernel,
        grid=(batch,),
        in_specs=[
            tok, tok,
            pl.BlockSpec((CONV_WIDTH, D_LRU), lambda b: (0, 0)),
            pl.BlockSpec((1, D_LRU), lambda b: (0, 0)),
            pl.BlockSpec(wg_bd.shape, lambda b: (0, 0, 0, 0, 0)),
            pl.BlockSpec((2, 2, D_LRU), lambda b: (0, 0, 0)),
            pl.BlockSpec((2, D_LRU), lambda b: (0, 0)),
        ],
        out_specs=tok,
        out_shape=jax.ShapeDtypeStruct((n, D_LRU), _BF16),
        scratch_shapes=[scan_buf] * 4 + [pltpu.VMEM((seq, D_LRU), _F32)],
        compiler_params=_params("parallel"),
    )(xr, gy, conv_w, conv_b.reshape(1, D_LRU), wg_bd, b_gate, lam)


def _block_diag_gates(w_gate):
    per_tile = MXU_DIM // LRU_BLOCK
    w = w_gate.reshape(2, 2, N_LRU_BLOCKS // per_tile, per_tile, LRU_BLOCK, LRU_BLOCK)
    eye = jnp.eye(per_tile, dtype=w.dtype)
    bd = jnp.einsum('dghnij,nm->dghnimj', w, eye)
    return bd.reshape(2, 2, N_LRU_BLOCKS // per_tile, MXU_DIM, MXU_DIM)


def kernel(x, c, positions, w_ada, b_ada, norm_g, ffn1_w_in, ffn1_w_out, ffn2_w_in, ffn2_w_out,
           w_mix_in, w_mix_out, lambda_q, lambda_k, subln_g, conv_w, conv_b,
           lru_w_gate, lru_b_gate, lru_lambda):
    batch, seq, d = x.shape
    depth = w_ada.shape[0]
    assert (d, seq % TOKEN_TILE, seq % Q_TILE) == (D_MODEL, 0, 0)
    assert seq == SCAN_SEG * SCAN_NSEG

    ada = _ada_all_layers(c, w_ada, b_ada).reshape(depth, batch, 3 * N_SUB, d)
    cos_t, sin_t = _rope_tables(positions)
    x2 = x.reshape(batch * seq, d)
    ffn1_in, ffn1_out, ffn2_in, ffn2_out, mix_in_w, mix_out_w = (
        w.astype(_BF16) for w in (ffn1_w_in, ffn1_w_out, ffn2_w_in, ffn2_w_out, w_mix_in, w_mix_out))

    for l in range(depth):
        lambda_init = 0.8 - 0.6 * math.exp(-0.3 * l)
        x2 = _ffn_sublayer(x2, ada, norm_g, ffn1_in, ffn1_out, l=l, j=0, res_w=0.5, seq=seq)
        q, k, v, xr, gy = _mix_in(x2, ada, norm_g, mix_in_w, cos_t, sin_t, l=l, seq=seq)
        attn = _attention(q, k, v, lambda_q[l], lambda_k[l], subln_g[l],
                          lambda_init=lambda_init, batch=batch, seq=seq)
        rec = _recurrent_group(xr, gy, conv_w[l], conv_b[l],
                               (0.5 * _block_diag_gates(lru_w_gate[l])).astype(_BF16),
                               lru_b_gate[l], lru_lambda[l], batch=batch, seq=seq)
        x2 = _ffn_sublayer(x2, ada, norm_g, ffn2_in, ffn2_out, l=l, j=2, res_w=0.5, seq=seq,
                           mixer=(attn, rec, mix_out_w))
    return x2.reshape(batch, seq, d)
```

```python
import functools
import math

import jax
import jax.numpy as jnp
from jax import lax
from jax.experimental import pallas as pl
from jax.experimental.pallas import tpu as pltpu

D_MODEL = 1024
D_ATTN = 512
D_LRU = 512
HEAD_DIM = 64
V_DIM = 128
N_ATTN_HEADS = 4
N_LRU_BLOCKS = 8
LRU_BLOCK = 64
CONV_WIDTH = 4
LRU_C = 8.0
MIX_IN = 3 * D_ATTN + 2 * D_LRU
D_FF = 2816
ROPE_THETA = 10000.0
EPS = 1e-6
N_SUB = 3
LOG2_E = 1.4426950408889634

LANES = 128
SUBLANES = 8
MXU_DIM = 256
VMEM_LIMIT_BYTES = 56 * 1024 * 1024

TOKEN_TILE = 1024
FFN_SUB_TILE = 512
Q_TILE = 512
ATTN_HEADS_PER_STEP = 1
FF_CHUNK = MXU_DIM
ADA_COLS = 2304

SCAN_SEG = 64
SCAN_NSEG = 32
SCAN_PITCH = 68
LANE_GROUPS = D_LRU // LANES

_F32 = jnp.float32
_BF16 = jnp.bfloat16


def _params(*sem):
    return pltpu.CompilerParams(dimension_semantics=sem, vmem_limit_bytes=VMEM_LIMIT_BYTES)


def _rms(x):
    return lax.rsqrt(jnp.mean(x * x, axis=-1, keepdims=True) + EPS)


def _ada_rows(ada_ref, j):
    shift = ada_ref[0, 3 * j:3 * j + 1, :]
    scale = ada_ref[0, 3 * j + 1:3 * j + 2, :]
    gate = ada_ref[0, 3 * j + 2:3 * j + 3, :]
    return shift, scale, gate


def _prenorm(x, g_pre, scale, shift):
    return (x * _rms(x)) * (g_pre * (1.0 + scale)) + shift


def _ada_kernel(c_ref, w_ref, b_ref, o_ref):
    c = c_ref[...]
    cond = (c * jax.nn.sigmoid(c)).astype(_BF16)
    w = w_ref[0].astype(_BF16)
    o_ref[0] = jnp.dot(cond, w, preferred_element_type=_F32) + b_ref[0]


def _ada_all_layers(c, w_ada, b_ada):
    depth, d, n = w_ada.shape
    b = c.shape[0]
    return pl.pallas_call(
        _ada_kernel,
        grid=(depth, n // ADA_COLS),
        in_specs=[
            pl.BlockSpec((b, d), lambda l, i: (0, 0)),
            pl.BlockSpec((1, d, ADA_COLS), lambda l, i: (l, 0, i)),
            pl.BlockSpec((1, 1, ADA_COLS), lambda l, i: (l, 0, i)),
        ],
        out_specs=pl.BlockSpec((1, b, ADA_COLS), lambda l, i: (l, 0, i)),
        out_shape=jax.ShapeDtypeStruct((depth, b, n), _F32),
        compiler_params=_params("parallel", "parallel"),
    )(c, w_ada, b_ada.reshape(depth, 1, n))


def _rope_kernel(pos_ref, inv_ref, sign_ref, cos_ref, sin_ref):
    ang = pos_ref[...].astype(_F32) * inv_ref[...]
    cos_ref[...] = jnp.cos(ang)
    sin_ref[...] = jnp.sin(ang) * sign_ref[...]


def _rope_tables(positions):
    n = positions.size
    inv = ROPE_THETA ** (-jnp.arange(0, HEAD_DIM, 2, dtype=_F32) / HEAD_DIM)
    reps = LANES // (HEAD_DIM // 2)
    inv_row = jnp.tile(inv, reps).reshape(1, LANES)
    half = jnp.concatenate([-jnp.ones((HEAD_DIM // 2,), _F32), jnp.ones((HEAD_DIM // 2,), _F32)])
    sign_row = jnp.tile(half, LANES // HEAD_DIM).reshape(1, LANES)
    row = pl.BlockSpec((1, LANES), lambda i: (0, 0))
    tab = pl.BlockSpec((TOKEN_TILE, LANES), lambda i: (i, 0))
    return pl.pallas_call(
        _rope_kernel,
        grid=(n // TOKEN_TILE,),
        in_specs=[pl.BlockSpec((TOKEN_TILE, 1), lambda i: (i, 0)), row, row],
        out_specs=[tab, tab],
        out_shape=[jax.ShapeDtypeStruct((n, LANES), _F32)] * 2,
        compiler_params=_params("parallel"),
    )(positions.reshape(n, 1), inv_row, sign_row)


def _ffn_kernel(*refs, j, res_w, after_mixer):
    if after_mixer:
        x_ref, attn_ref, rec_ref, wmix_ref, ada_ref, g_ref, win_ref, wout_ref, o_ref, act_ref = refs
    else:
        x_ref, ada_ref, g_ref, win_ref, wout_ref, o_ref, act_ref = refs
    shift, scale, gate = _ada_rows(ada_ref, j)
    for r in range(TOKEN_TILE // FFN_SUB_TILE):
        rows = slice(r * FFN_SUB_TILE, (r + 1) * FFN_SUB_TILE)
        x = x_ref[rows, :]
        if after_mixer:
            _, _, mix_gate = _ada_rows(ada_ref, 1)
            m = (jnp.dot(attn_ref[rows, :], wmix_ref[:D_ATTN, :].astype(_BF16), preferred_element_type=_F32)
                 + jnp.dot(rec_ref[rows, :], wmix_ref[D_ATTN:, :].astype(_BF16), preferred_element_type=_F32))
            x = x + mix_gate * ((m * _rms(m)) * g_ref[3:4, :])
        h = _prenorm(x, g_ref[2 * j:2 * j + 1, :], scale, shift).astype(_BF16)
        for c in range(D_FF // FF_CHUNK):
            lo = c * FF_CHUNK
            g = jnp.dot(h, win_ref[:, lo:lo + FF_CHUNK], preferred_element_type=_F32)
            u = jnp.dot(h, win_ref[:, D_FF + lo:D_FF + lo + FF_CHUNK], preferred_element_type=_F32)
            act_ref[rows, lo:lo + FF_CHUNK] = (g * jax.nn.sigmoid(g) * u).astype(_BF16)
        y = jnp.dot(act_ref[rows, :], wout_ref[...], preferred_element_type=_F32)
        y = (y * _rms(y)) * g_ref[2 * j + 1:2 * j + 2, :]
        o_ref[rows, :] = x + (res_w * gate) * y


def _layer_block(shape, l, **kw):
    zeros = (0,) * len(shape)
    return pl.BlockSpec((None,) + tuple(shape), lambda i: (l,) + zeros, **kw)


def _ada_block(l, tiles_per_seq):
    return pl.BlockSpec((None, 1, 3 * N_SUB, D_MODEL), lambda i: (l, i // tiles_per_seq, 0, 0))


def _ffn_sublayer(x2, ada, norm_g, w_in, w_out, *, l, j, res_w, seq, mixer=None):
    n, d = x2.shape
    tiles_per_seq = seq // TOKEN_TILE
    resident = dict(pipeline_mode=pl.Buffered(1))
    tok = pl.BlockSpec((TOKEN_TILE, d), lambda i: (i, 0))
    in_specs, args = [tok], [x2]
    if mixer is not None:
        col = pl.BlockSpec((TOKEN_TILE, D_ATTN), lambda i: (i, 0))
        in_specs += [col, col, _layer_block((d, d), l, **resident)]
        args += list(mixer)
    in_specs += [
        _ada_block(l, tiles_per_seq),
        _layer_block((2 * N_SUB, d), l),
        _layer_block((d, 2 * D_FF), l, **resident),
        _layer_block((D_FF, d), l, **resident),
    ]
    args += [ada, norm_g, w_in, w_out]
    return pl.pallas_call(
        functools.partial(_ffn_kernel, j=j, res_w=res_w, after_mixer=mixer is not None),
        grid=(n // TOKEN_TILE,),
        in_specs=in_specs,
        out_specs=tok,
        out_shape=jax.ShapeDtypeStruct((n, d), _F32),
        scratch_shapes=[pltpu.VMEM((TOKEN_TILE, D_FF), _BF16)],
        compiler_params=_params("parallel"),
    )(*args)


def _rope(t, cos, sin_signed):
    lane = lax.broadcasted_iota(jnp.int32, t.shape, 1)
    first_half = (lane % HEAD_DIM) < (HEAD_DIM // 2)
    partner = jnp.where(first_half,
                        pltpu.roll(t, LANES - HEAD_DIM // 2, axis=1),
                        pltpu.roll(t, HEAD_DIM // 2, axis=1))
    return t * cos + partner * sin_signed


def _mix_in_kernel(x_ref, ada_ref, g_ref, w_ref, cos_ref, sin_ref,
                   q_ref, k_ref, v_ref, xr_ref, gy_ref):
    j = 1
    x = x_ref[...]
    shift, scale, _ = _ada_rows(ada_ref, j)
    h = _prenorm(x, g_ref[2 * j:2 * j + 1, :], scale, shift).astype(_BF16)
    cos = cos_ref[...]
    sin = sin_ref[...]
    q_scale = HEAD_DIM ** -0.5 * LOG2_E

    def proj(lo, width):
        return jnp.dot(h, w_ref[:, lo:lo + width].astype(_BF16), preferred_element_type=_F32)

    q = proj(0, D_ATTN)
    k = proj(D_ATTN, D_ATTN)
    for hd in range(N_ATTN_HEADS):
        head = slice(hd * LANES, (hd + 1) * LANES)
        q_ref[:, head] = (_rope(q[:, head], cos, sin) * q_scale).astype(_BF16)
        k_ref[:, head] = _rope(k[:, head], cos, sin).astype(_BF16)
    v_ref[...] = proj(2 * D_ATTN, D_ATTN).astype(_BF16)
    xr_ref[...] = proj(3 * D_ATTN, D_LRU)
    gy_ref[...] = jax.nn.gelu(proj(3 * D_ATTN + D_LRU, D_LRU))


def _mix_in(x2, ada, norm_g, w_in, cos_t, sin_t, *, l, seq):
    n, d = x2.shape
    tiles_per_seq = seq // TOKEN_TILE
    half = lambda dt: jax.ShapeDtypeStruct((n, D_ATTN), dt)
    col = pl.BlockSpec((TOKEN_TILE, D_ATTN), lambda i: (i, 0))
    tab = pl.BlockSpec((TOKEN_TILE, LANES), lambda i: (i, 0))
    return pl.pallas_call(
        _mix_in_kernel,
        grid=(n // TOKEN_TILE,),
        in_specs=[
            pl.BlockSpec((TOKEN_TILE, d), lambda i: (i, 0)),
            _ada_block(l, tiles_per_seq),
            _layer_block((2 * N_SUB, d), l),
            _layer_block((d, MIX_IN), l, pipeline_mode=pl.Buffered(1)),
            tab, tab,
        ],
        out_specs=[col] * 5,
        out_shape=[half(_BF16), half(_BF16), half(_BF16), half(_F32), half(_F32)],
        compiler_params=_params("parallel"),
    )(x2, ada, norm_g, w_in, cos_t, sin_t)


def _store_scores(q, k, s_ref, m_ref):
    lane = lax.broadcasted_iota(jnp.int32, q.shape, 1)
    first_map = lane < HEAD_DIM
    zero = jnp.zeros_like(q)
    nt = (((1,), (1,)), ((), ()))
    for m, qm in enumerate((jnp.where(first_map, q, zero), jnp.where(first_map, zero, q))):
        s = lax.dot_general(qm, k, nt, preferred_element_type=_F32)
        s_ref[m] = s
        m_ref[m] = jnp.max(s, axis=-1, keepdims=True)


def _diff_softmax_pv(s_ref, m_ref, v_ones, lam, out_gain):
    maps = []
    for m in range(2):
        p = jnp.exp2(s_ref[m] - m_ref[m])
        pv = jnp.dot(p.astype(_BF16), v_ones, preferred_element_type=_F32)
        maps.append(pv[:, :V_DIM] * (1.0 / pv[:, V_DIM:]))
    o = maps[0] - lam * maps[1]
    return (o * _rms(o)) * out_gain


def _attn_kernel(q_ref, qn_ref, k_ref, kn_ref, v_ref, lq_ref, lk_ref, sg_ref, o_ref,
                 s0_ref, s1_ref, m0_ref, m1_ref, *, lambda_init):
    q_tiles = q_ref.shape[0] // Q_TILE
    scores = ((s0_ref, m0_ref), (s1_ref, m1_ref))
    tiles = [(slice(i * Q_TILE, (i + 1) * Q_TILE), slice(hd * V_DIM, (hd + 1) * V_DIM))
             for hd in range(ATTN_HEADS_PER_STEP) for i in range(q_tiles)]

    @pl.when(pl.program_id(0) == 0)
    def _():
        rows, head = tiles[0]
        _store_scores(q_ref[rows, head], k_ref[:, head], *scores[0])

    lqk = lq_ref[...] * lk_ref[...]
    lam = (jnp.exp(jnp.sum(lqk[0:1, :], axis=-1, keepdims=True))
           - jnp.exp(jnp.sum(lqk[1:2, :], axis=-1, keepdims=True)) + lambda_init)
    out_gain = sg_ref[...] * (1.0 - lambda_init)

    for i, (rows, head) in enumerate(tiles):
        if i + 1 < len(tiles):
            next_rows, next_head = tiles[i + 1]
            _store_scores(q_ref[next_rows, next_head], k_ref[:, next_head], *scores[(i + 1) % 2])
        else:
            _store_scores(qn_ref[...], kn_ref[...], *scores[(i + 1) % 2])
        v_ones = jnp.concatenate([v_ref[:, head], jnp.ones((v_ref.shape[0], V_DIM), _BF16)], axis=1)
        o_ref[rows, head] = _diff_softmax_pv(*scores[i % 2], v_ones, lam, out_gain).astype(o_ref.dtype)


def _attention(q, k, v, lq, lk, sg, *, lambda_init, batch, seq):
    n = q.shape[0]
    q_tiles = seq // Q_TILE
    assert (q_tiles * ATTN_HEADS_PER_STEP) % 2 == 0
    groups = N_ATTN_HEADS // ATTN_HEADS_PER_STEP
    steps = batch * groups

    def group_block(t):
        return t // groups, t % groups

    def next_step(t):
        return jnp.minimum(t + 1, steps - 1)

    def next_first_tile(t):
        b, g = group_block(next_step(t))
        return b * q_tiles, g * ATTN_HEADS_PER_STEP

    def next_first_head(t):
        b, g = group_block(next_step(t))
        return b, g * ATTN_HEADS_PER_STEP

    whole = pl.BlockSpec((seq, ATTN_HEADS_PER_STEP * V_DIM), group_block)
    small = lambda shape: pl.BlockSpec(shape, lambda t: (0, 0))
    scores = pltpu.VMEM((2, Q_TILE, seq), _F32)
    row_max = pltpu.VMEM((2, Q_TILE, 1), _F32)
    return pl.pallas_call(
        functools.partial(_attn_kernel, lambda_init=lambda_init),
        grid=(steps,),
        in_specs=[
            whole,
            pl.BlockSpec((Q_TILE, V_DIM), next_first_tile),
            whole,
            pl.BlockSpec((seq, V_DIM), next_first_head),
            whole,
            small((2, HEAD_DIM)), small((2, HEAD_DIM)), small((1, V_DIM)),
        ],
        out_specs=whole,
        out_shape=jax.ShapeDtypeStruct((n, D_ATTN), _BF16),
        scratch_shapes=[scores, scores, row_max, row_max],
        compiler_params=_params("arbitrary"),
    )(q, q, k, k, v, lq, lk, sg.reshape(1, V_DIM))


def _scan_direction(a_scr, b_scr, p_scr, h_scr, *, reverse):
    chains = [(lg, sg) for lg in range(LANE_GROUPS) for sg in range(SCAN_NSEG // SUBLANES)]

    def seg_rows(sg, j):
        return pl.ds(sg * SUBLANES * SCAN_PITCH + j, SUBLANES, stride=SCAN_PITCH)

    def step(jj, carry):
        j = (SCAN_SEG - 1 - jj) if reverse else jj
        hs, ps = carry
        new_h, new_p = [], []
        for c, (lg, sg) in enumerate(chains):
            a = a_scr[lg, seg_rows(sg, j), :]
            b = b_scr[lg, seg_rows(sg, j), :]
            h = a * hs[c] + b
            p = a * ps[c]
            h_scr[lg, seg_rows(sg, j), :] = h
            p_scr[lg, seg_rows(sg, j), :] = p
            new_h.append(h)
            new_p.append(p)
        return tuple(new_h), tuple(new_p)

    zeros = tuple(jnp.zeros((SUBLANES, LANES), _F32) for _ in chains)
    ones = tuple(jnp.ones((SUBLANES, LANES), _F32) for _ in chains)
    h_end, p_tot = lax.fori_loop(0, SCAN_SEG, step, (zeros, ones), unroll=8)

    order = range(SCAN_NSEG - 1, -1, -1) if reverse else range(SCAN_NSEG)
    carries = []
    for lg in range(LANE_GROUPS):
        carry_in = jnp.zeros((1, LANES), _F32)
        per_seg = [None] * SCAN_NSEG
        for s in order:
            c = chains.index((lg, s // SUBLANES))
            r = s % SUBLANES
            per_seg[s] = carry_in
            carry_in = p_tot[c][r:r + 1, :] * carry_in + h_end[c][r:r + 1, :]
        carries.append(per_seg)
    return carries


def _depthwise_conv(x_ref, cw, cb):
    seq = x_ref.shape[0]

    def taps(x, m2, m1, p1):
        return cb + cw[2:3, :] * x + cw[0:1, :] * m2 + cw[1:2, :] * m1 + cw[3:4, :] * p1

    row = lax.broadcasted_iota(jnp.int32, (SUBLANES, x_ref.shape[1]), 0)
    lo, hi = SUBLANES, seq - SUBLANES
    x_top = x_ref[0:lo, :]
    x_bot = x_ref[hi:seq, :]
    return jnp.concatenate([
        taps(x_top,
             jnp.where(row >= 2, pltpu.roll(x_top, 2, axis=0), 0.0),
             jnp.where(row >= 1, pltpu.roll(x_top, 1, axis=0), 0.0),
             x_ref[1:lo + 1, :]),
        taps(x_ref[lo:hi, :], x_ref[lo - 2:hi - 2, :], x_ref[lo - 1:hi - 1, :], x_ref[lo + 1:hi + 1, :]),
        taps(x_bot, x_ref[hi - 2:seq - 2, :], x_ref[hi - 1:seq - 1, :],
             jnp.where(row < SUBLANES - 1, pltpu.roll(x_bot, SUBLANES - 1, axis=0), 0.0)),
    ], axis=0)


def _lru_kernel(xr_ref, gy_ref, cw_ref, cb_ref, wg_ref, bg_ref, lam_ref, o_ref,
                a_scr, b_scr, p_scr, h_scr, sum_scr):
    xc = _depthwise_conv(xr_ref, cw_ref[...], cb_ref[...])
    xcb = xc.astype(_BF16)
    xc_half = 0.5 * xc

    for d in range(2):
        lam = lam_ref[d:d + 1, :]
        half_c = (-0.5 * LRU_C) * (jnp.maximum(-lam, 0.0) + jnp.log1p(jnp.exp(-jnp.abs(lam))))
        for half in range(D_LRU // MXU_DIM):
            cols = slice(half * MXU_DIM, (half + 1) * MXU_DIM)
            xh = xcb[:, cols]
            gr = jnp.dot(xh, wg_ref[d, 0, half], preferred_element_type=_F32)
            gi = jnp.dot(xh, wg_ref[d, 1, half], preferred_element_type=_F32)
            t_r = jnp.tanh(gr + 0.5 * bg_ref[d, 0:1, cols])
            t_i = jnp.tanh(gi + 0.5 * bg_ref[d, 1:2, cols])
            log_a = half_c[:, cols] * t_r + half_c[:, cols]
            a = jnp.exp(log_a)
            one_minus_a2 = jnp.tanh(log_a) * (-1.0 - a * a)
            mult = jnp.where(one_minus_a2 > 0.0, one_minus_a2 * lax.rsqrt(one_minus_a2), 0.0)
            b = mult * ((t_i + 1.0) * xc_half[:, cols])
            for sub in range(MXU_DIM // LANES):
                lg = half * (MXU_DIM // LANES) + sub
                lanes = slice(sub * LANES, (sub + 1) * LANES)
                for s in range(SCAN_NSEG):
                    src = slice(s * SCAN_SEG, (s + 1) * SCAN_SEG)
                    dst = pl.ds(s * SCAN_PITCH, SCAN_SEG)
                    a_scr[lg, dst, :] = a[src, lanes]
                    b_scr[lg, dst, :] = b[src, lanes]
        carries = _scan_direction(a_scr, b_scr, p_scr, h_scr, reverse=(d == 1))
        for lg in range(LANE_GROUPS):
            for s in range(SCAN_NSEG):
                rows = pl.ds(s * SCAN_PITCH, SCAN_SEG)
                dst = (slice(s * SCAN_SEG, (s + 1) * SCAN_SEG), slice(lg * LANES, (lg + 1) * LANES))
                hseg = h_scr[lg, rows, :] + p_scr[lg, rows, :] * carries[lg][s]
                if d == 0:
                    sum_scr[dst] = hseg
                else:
                    o_ref[dst] = ((sum_scr[dst] + hseg) * gy_ref[dst]).astype(o_ref.dtype)


def _recurrent_group(xr, gy, conv_w, conv_b, wg_bd, b_gate, lam, *, batch, seq):
    n = xr.shape[0]
    tok = pl.BlockSpec((seq, D_LRU), lambda b: (b, 0))
    scan_buf = pltpu.VMEM((LANE_GROUPS, SCAN_NSEG * SCAN_PITCH, LANES), _F32)
    return pl.pallas_call(
        _lru_kernel,
        grid=(batch,),
        in_specs=[
            tok, tok,
            pl.BlockSpec((CONV_WIDTH, D_LRU), lambda b: (0, 0)),
            pl.BlockSpec((1, D_LRU), lambda b: (0, 0)),
            pl.BlockSpec(wg_bd.shape, lambda b: (0, 0, 0, 0, 0)),
            pl.BlockSpec((2, 2, D_LRU), lambda b: (0, 0, 0)),
            pl.BlockSpec((2, D_LRU), lambda b: (0, 0)),
        ],
        out_specs=tok,
        out_shape=jax.ShapeDtypeStruct((n, D_LRU), _BF16),
        scratch_shapes=[scan_buf] * 4 + [pltpu.VMEM((seq, D_LRU), _F32)],
        compiler_params=_params("parallel"),
    )(xr, gy, conv_w, conv_b.reshape(1, D_LRU), wg_bd, b_gate, lam)


def _block_diag_gates(w_gate):
    per_tile = MXU_DIM // LRU_BLOCK
    w = w_gate.reshape(2, 2, N_LRU_BLOCKS // per_tile, per_tile, LRU_BLOCK, LRU_BLOCK)
    eye = jnp.eye(per_tile, dtype=w.dtype)
    bd = jnp.einsum('dghnij,nm->dghnimj', w, eye)
    return bd.reshape(2, 2, N_LRU_BLOCKS // per_tile, MXU_DIM, MXU_DIM)


def kernel(x, c, positions, w_ada, b_ada, norm_g, ffn1_w_in, ffn1_w_out, ffn2_w_in, ffn2_w_out,
           w_mix_in, w_mix_out, lambda_q, lambda_k, subln_g, conv_w, conv_b,
           lru_w_gate, lru_b_gate, lru_lambda):
    batch, seq, d = x.shape
    depth = w_ada.shape[0]
    assert (d, seq % TOKEN_TILE, seq % Q_TILE) == (D_MODEL, 0, 0)
    assert seq == SCAN_SEG * SCAN_NSEG

    ada = _ada_all_layers(c, w_ada, b_ada).reshape(depth, batch, 3 * N_SUB, d)
    cos_t, sin_t = _rope_tables(positions)
    x2 = x.reshape(batch * seq, d)
    ffn1_in, ffn1_out, ffn2_in, ffn2_out = (
        w.astype(_BF16) for w in (ffn1_w_in, ffn1_w_out, ffn2_w_in, ffn2_w_out))

    for l in range(depth):
        lambda_init = 0.8 - 0.6 * math.exp(-0.3 * l)
        x2 = _ffn_sublayer(x2, ada, norm_g, ffn1_in, ffn1_out, l=l, j=0, res_w=0.5, seq=seq)
        q, k, v, xr, gy = _mix_in(x2, ada, norm_g, w_mix_in, cos_t, sin_t, l=l, seq=seq)
        attn = _attention(q, k, v, lambda_q[l], lambda_k[l], subln_g[l],
                          lambda_init=lambda_init, batch=batch, seq=seq)
        rec = _recurrent_group(xr, gy, conv_w[l], conv_b[l],
                               (0.5 * _block_diag_gates(lru_w_gate[l])).astype(_BF16),
                               lru_b_gate[l], lru_lambda[l], batch=batch, seq=seq)
        x2 = _ffn_sublayer(x2, ada, norm_g, ffn2_in, ffn2_out, l=l, j=2, res_w=0.5, seq=seq,
                           mixer=(attn, rec, w_mix_out))
    return x2.reshape(batch, seq, d)
```

```python
import functools
import math

import jax
import jax.numpy as jnp
from jax import lax
from jax.experimental import pallas as pl
from jax.experimental.pallas import tpu as pltpu

D_MODEL = 1024
D_ATTN = 512
D_LRU = 512
HEAD_DIM = 64
V_DIM = 128
N_ATTN_HEADS = 4
N_LRU_BLOCKS = 8
LRU_BLOCK = 64
CONV_WIDTH = 4
LRU_C = 8.0
MIX_IN = 3 * D_ATTN + 2 * D_LRU
D_FF = 2816
ROPE_THETA = 10000.0
EPS = 1e-6
N_SUB = 3
LOG2_E = 1.4426950408889634

LANES = 128
SUBLANES = 8
BF16_SUBLANES = 16
MXU_DIM = 256
VMEM_LIMIT_BYTES = 56 * 1024 * 1024

TOKEN_TILE = 1024
FFN_SUB_TILE = 512
Q_TILE = 512
ATTN_HEADS_PER_STEP = 1
FF_CHUNK = MXU_DIM
ADA_COLS = 2304

SCAN_SEG = 64
SCAN_NSEG = 32
SCAN_PITCH = 68
LANE_GROUPS = D_LRU // LANES

_F32 = jnp.float32
_BF16 = jnp.bfloat16


def _params(*sem):
    return pltpu.CompilerParams(dimension_semantics=sem, vmem_limit_bytes=VMEM_LIMIT_BYTES)


def _rms(x):
    return lax.rsqrt(jnp.mean(x * x, axis=-1, keepdims=True) + EPS)


def _ada_rows(ada_ref, j):
    shift = ada_ref[0, 3 * j:3 * j + 1, :]
    scale = ada_ref[0, 3 * j + 1:3 * j + 2, :]
    gate = ada_ref[0, 3 * j + 2:3 * j + 3, :]
    return shift, scale, gate


def _prenorm(x, g_pre, scale, shift):
    return (x * _rms(x)) * (g_pre * (1.0 + scale)) + shift


def _ada_kernel(c_ref, w_ref, b_ref, o_ref):
    c = c_ref[...]
    cond = (c * jax.nn.sigmoid(c)).astype(_BF16)
    w = w_ref[0].astype(_BF16)
    o_ref[0] = jnp.dot(cond, w, preferred_element_type=_F32) + b_ref[0]


def _ada_all_layers(c, w_ada, b_ada):
    depth, d, n = w_ada.shape
    b = c.shape[0]
    return pl.pallas_call(
        _ada_kernel,
        grid=(depth, n // ADA_COLS),
        in_specs=[
            pl.BlockSpec((b, d), lambda l, i: (0, 0)),
            pl.BlockSpec((1, d, ADA_COLS), lambda l, i: (l, 0, i)),
            pl.BlockSpec((1, 1, ADA_COLS), lambda l, i: (l, 0, i)),
        ],
        out_specs=pl.BlockSpec((1, b, ADA_COLS), lambda l, i: (l, 0, i)),
        out_shape=jax.ShapeDtypeStruct((depth, b, n), _F32),
        compiler_params=_params("parallel", "parallel"),
    )(c, w_ada, b_ada.reshape(depth, 1, n))


def _rope_kernel(pos_ref, inv_ref, sign_ref, cos_ref, sin_ref):
    ang = pos_ref[...].astype(_F32) * inv_ref[...]
    cos_ref[...] = jnp.cos(ang)
    sin_ref[...] = jnp.sin(ang) * sign_ref[...]


def _rope_tables(positions):
    n = positions.size
    inv = ROPE_THETA ** (-jnp.arange(0, HEAD_DIM, 2, dtype=_F32) / HEAD_DIM)
    reps = LANES // (HEAD_DIM // 2)
    inv_row = jnp.tile(inv, reps).reshape(1, LANES)
    half = jnp.concatenate([-jnp.ones((HEAD_DIM // 2,), _F32), jnp.ones((HEAD_DIM // 2,), _F32)])
    sign_row = jnp.tile(half, LANES // HEAD_DIM).reshape(1, LANES)
    row = pl.BlockSpec((1, LANES), lambda i: (0, 0))
    tab = pl.BlockSpec((TOKEN_TILE, LANES), lambda i: (i, 0))
    return pl.pallas_call(
        _rope_kernel,
        grid=(n // TOKEN_TILE,),
        in_specs=[pl.BlockSpec((TOKEN_TILE, 1), lambda i: (i, 0)), row, row],
        out_specs=[tab, tab],
        out_shape=[jax.ShapeDtypeStruct((n, LANES), _F32)] * 2,
        compiler_params=_params("parallel"),
    )(positions.reshape(n, 1), inv_row, sign_row)


def _ffn_kernel(*refs, j, res_w, after_mixer):
    if after_mixer:
        x_ref, attn_ref, rec_ref, wmix_ref, ada_ref, g_ref, win_ref, wout_ref, o_ref, act_ref = refs
    else:
        x_ref, ada_ref, g_ref, win_ref, wout_ref, o_ref, act_ref = refs
    shift, scale, gate = _ada_rows(ada_ref, j)
    for r in range(TOKEN_TILE // FFN_SUB_TILE):
        rows = slice(r * FFN_SUB_TILE, (r + 1) * FFN_SUB_TILE)
        x = x_ref[rows, :]
        if after_mixer:
            _, _, mix_gate = _ada_rows(ada_ref, 1)
            m = (jnp.dot(attn_ref[rows, :], wmix_ref[:D_ATTN, :].astype(_BF16), preferred_element_type=_F32)
                 + jnp.dot(rec_ref[rows, :], wmix_ref[D_ATTN:, :].astype(_BF16), preferred_element_type=_F32))
            x = x + mix_gate * ((m * _rms(m)) * g_ref[3:4, :])
        h = _prenorm(x, g_ref[2 * j:2 * j + 1, :], scale, shift).astype(_BF16)
        for c in range(D_FF // FF_CHUNK):
            lo = c * FF_CHUNK
            g = jnp.dot(h, win_ref[:, lo:lo + FF_CHUNK], preferred_element_type=_F32)
            u = jnp.dot(h, win_ref[:, D_FF + lo:D_FF + lo + FF_CHUNK], preferred_element_type=_F32)
            act_ref[rows, lo:lo + FF_CHUNK] = (g * jax.nn.sigmoid(g) * u).astype(_BF16)
        y = jnp.dot(act_ref[rows, :], wout_ref[...], preferred_element_type=_F32)
        y = (y * _rms(y)) * g_ref[2 * j + 1:2 * j + 2, :]
        o_ref[rows, :] = x + (res_w * gate) * y


def _layer_block(shape, l, **kw):
    zeros = (0,) * len(shape)
    return pl.BlockSpec((None,) + tuple(shape), lambda i: (l,) + zeros, **kw)


def _ada_block(l, tiles_per_seq):
    return pl.BlockSpec((None, 1, 3 * N_SUB, D_MODEL), lambda i: (l, i // tiles_per_seq, 0, 0))


def _ffn_sublayer(x2, ada, norm_g, w_in, w_out, *, l, j, res_w, seq, mixer=None):
    n, d = x2.shape
    tiles_per_seq = seq // TOKEN_TILE
    resident = dict(pipeline_mode=pl.Buffered(1))
    tok = pl.BlockSpec((TOKEN_TILE, d), lambda i: (i, 0))
    in_specs, args = [tok], [x2]
    if mixer is not None:
        col = pl.BlockSpec((TOKEN_TILE, D_ATTN), lambda i: (i, 0))
        in_specs += [col, col, _layer_block((d, d), l, **resident)]
        args += list(mixer)
    in_specs += [
        _ada_block(l, tiles_per_seq),
        _layer_block((2 * N_SUB, d), l),
        pl.BlockSpec((d, 2 * D_FF), lambda i: (0, 0), **resident),
        pl.BlockSpec((D_FF, d), lambda i: (0, 0), **resident),
    ]
    args += [ada, norm_g, w_in, w_out]
    return pl.pallas_call(
        functools.partial(_ffn_kernel, j=j, res_w=res_w, after_mixer=mixer is not None),
        grid=(n // TOKEN_TILE,),
        in_specs=in_specs,
        out_specs=tok,
        out_shape=jax.ShapeDtypeStruct((n, d), _F32),
        scratch_shapes=[pltpu.VMEM((TOKEN_TILE, D_FF), _BF16)],
        compiler_params=_params("parallel"),
    )(*args)


def _rope(t, cos, sin_signed):
    lane = lax.broadcasted_iota(jnp.int32, t.shape, 1)
    first_half = (lane % HEAD_DIM) < (HEAD_DIM // 2)
    partner = jnp.where(first_half,
                        pltpu.roll(t, LANES - HEAD_DIM // 2, axis=1),
                        pltpu.roll(t, HEAD_DIM // 2, axis=1))
    return t * cos + partner * sin_signed


def _mix_in_kernel(x_ref, ada_ref, g_ref, w_ref, cos_ref, sin_ref,
                   q_ref, k_ref, v_ref, xr_ref, gy_ref):
    j = 1
    x = x_ref[...]
    shift, scale, _ = _ada_rows(ada_ref, j)
    h = _prenorm(x, g_ref[2 * j:2 * j + 1, :], scale, shift).astype(_BF16)
    cos = cos_ref[...]
    sin = sin_ref[...]
    q_scale = HEAD_DIM ** -0.5 * LOG2_E

    def proj(lo, width):
        return jnp.dot(h, w_ref[:, lo:lo + width].astype(_BF16), preferred_element_type=_F32)

    q = proj(0, D_ATTN)
    k = proj(D_ATTN, D_ATTN)
    for hd in range(N_ATTN_HEADS):
        head = slice(hd * LANES, (hd + 1) * LANES)
        q_ref[:, head] = (_rope(q[:, head], cos, sin) * q_scale).astype(_BF16)
        k_ref[:, head] = _rope(k[:, head], cos, sin).astype(_BF16)
    v_ref[...] = proj(2 * D_ATTN, D_ATTN).astype(_BF16)
    xr_ref[...] = proj(3 * D_ATTN, D_LRU)
    gy_ref[...] = jax.nn.gelu(proj(3 * D_ATTN + D_LRU, D_LRU))


def _mix_in(x2, ada, norm_g, w_in, cos_t, sin_t, *, l, seq):
    n, d = x2.shape
    tiles_per_seq = seq // TOKEN_TILE
    half = lambda dt: jax.ShapeDtypeStruct((n, D_ATTN), dt)
    col = pl.BlockSpec((TOKEN_TILE, D_ATTN), lambda i: (i, 0))
    tab = pl.BlockSpec((TOKEN_TILE, LANES), lambda i: (i, 0))
    return pl.pallas_call(
        _mix_in_kernel,
        grid=(n // TOKEN_TILE,),
        in_specs=[
            pl.BlockSpec((TOKEN_TILE, d), lambda i: (i, 0)),
            _ada_block(l, tiles_per_seq),
            _layer_block((2 * N_SUB, d), l),
            _layer_block((d, MIX_IN), l, pipeline_mode=pl.Buffered(1)),
            tab, tab,
        ],
        out_specs=[col] * 5,
        out_shape=[half(_BF16), half(_BF16), half(_BF16), half(_F32), half(_F32)],
        compiler_params=_params("parallel"),
    )(x2, ada, norm_g, w_in, cos_t, sin_t)


def _store_scores(q, k, s_ref, m_ref):
    lane = lax.broadcasted_iota(jnp.int32, q.shape, 1)
    first_map = lane < HEAD_DIM
    zero = jnp.zeros_like(q)
    nt = (((1,), (1,)), ((), ()))
    for m, qm in enumerate((jnp.where(first_map, q, zero), jnp.where(first_map, zero, q))):
        s = lax.dot_general(qm, k, nt, preferred_element_type=_F32)
        s_ref[m] = s
        m_ref[m] = jnp.max(s, axis=-1, keepdims=True)


def _diff_softmax_pv(s_ref, m_ref, v_ones, lam, out_gain):
    maps = []
    for m in range(2):
        p = jnp.exp2(s_ref[m] - m_ref[m])
        pv = jnp.dot(p.astype(_BF16), v_ones, preferred_element_type=_F32)
        maps.append(pv[:, :V_DIM] * (1.0 / pv[:, V_DIM:]))
    o = maps[0] - lam * maps[1]
    return (o * _rms(o)) * out_gain


def _attn_kernel(*refs, lambda_init, n_round):
    q_ref, qn_ref, k_ref, kn_ref, v_ref, lq_ref, lk_ref, sg_ref = refs[:8]
    f32_weights = refs[8:8 + n_round]
    o_ref = refs[8 + n_round]
    bf16_weights = refs[9 + n_round:9 + 2 * n_round]
    s0_ref, s1_ref, m0_ref, m1_ref = refs[9 + 2 * n_round:]

    for src, dst in zip(f32_weights, bf16_weights):
        dst[...] = src[...].astype(dst.dtype)

    q_tiles = q_ref.shape[0] // Q_TILE
    scores = ((s0_ref, m0_ref), (s1_ref, m1_ref))
    tiles = [(slice(i * Q_TILE, (i + 1) * Q_TILE), slice(hd * V_DIM, (hd + 1) * V_DIM))
             for hd in range(ATTN_HEADS_PER_STEP) for i in range(q_tiles)]

    @pl.when(pl.program_id(0) == 0)
    def _():
        rows, head = tiles[0]
        _store_scores(q_ref[rows, head], k_ref[:, head], *scores[0])

    lqk = lq_ref[...] * lk_ref[...]
    lam = (jnp.exp(jnp.sum(lqk[0:1, :], axis=-1, keepdims=True))
           - jnp.exp(jnp.sum(lqk[1:2, :], axis=-1, keepdims=True)) + lambda_init)
    out_gain = sg_ref[...] * (1.0 - lambda_init)

    for i, (rows, head) in enumerate(tiles):
        if i + 1 < len(tiles):
            next_rows, next_head = tiles[i + 1]
            _store_scores(q_ref[next_rows, next_head], k_ref[:, next_head], *scores[(i + 1) % 2])
        else:
            _store_scores(qn_ref[...], kn_ref[...], *scores[(i + 1) % 2])
        v_ones = jnp.concatenate([v_ref[:, head], jnp.ones((v_ref.shape[0], V_DIM), _BF16)], axis=1)
        o_ref[rows, head] = _diff_softmax_pv(*scores[i % 2], v_ones, lam, out_gain).astype(o_ref.dtype)


def _row_chunks(rows, steps):
    chunks = steps
    while rows % chunks or (rows // chunks) % BF16_SUBLANES:
        chunks //= 2
    return chunks, steps // chunks


def _attention(q, k, v, lq, lk, sg, *, lambda_init, batch, seq, round_weights=()):
    n = q.shape[0]
    q_tiles = seq // Q_TILE
    assert (q_tiles * ATTN_HEADS_PER_STEP) % 2 == 0
    groups = N_ATTN_HEADS // ATTN_HEADS_PER_STEP
    steps = batch * groups

    def group_block(t):
        return t // groups, t % groups

    def next_step(t):
        return jnp.minimum(t + 1, steps - 1)

    def next_first_tile(t):
        b, g = group_block(next_step(t))
        return b * q_tiles, g * ATTN_HEADS_PER_STEP

    def next_first_head(t):
        b, g = group_block(next_step(t))
        return b, g * ATTN_HEADS_PER_STEP

    whole = pl.BlockSpec((seq, ATTN_HEADS_PER_STEP * V_DIM), group_block)
    small = lambda shape: pl.BlockSpec(shape, lambda t: (0, 0))
    scores = pltpu.VMEM((2, Q_TILE, seq), _F32)
    row_max = pltpu.VMEM((2, Q_TILE, 1), _F32)

    round_in, round_out, round_shapes = [], [], []
    for w, l in round_weights:
        rows, cols = w.shape[1:]
        chunks, rep = _row_chunks(rows, steps)
        round_in.append(pl.BlockSpec((None, rows // chunks, cols), lambda t, l=l, rep=rep: (l, t // rep, 0)))
        round_out.append(pl.BlockSpec((rows // chunks, cols), lambda t, rep=rep: (t // rep, 0)))
        round_shapes.append(jax.ShapeDtypeStruct((rows, cols), _BF16))

    return pl.pallas_call(
        functools.partial(_attn_kernel, lambda_init=lambda_init, n_round=len(round_weights)),
        grid=(steps,),
        in_specs=[
            whole,
            pl.BlockSpec((Q_TILE, V_DIM), next_first_tile),
            whole,
            pl.BlockSpec((seq, V_DIM), next_first_head),
            whole,
            small((2, HEAD_DIM)), small((2, HEAD_DIM)), small((1, V_DIM)),
        ] + round_in,
        out_specs=[whole] + round_out,
        out_shape=[jax.ShapeDtypeStruct((n, D_ATTN), _BF16)] + round_shapes,
        scratch_shapes=[scores, scores, row_max, row_max],
        compiler_params=_params("arbitrary"),
    )(q, q, k, k, v, lq, lk, sg.reshape(1, V_DIM), *[w for w, _ in round_weights])


def _scan_direction(a_scr, b_scr, p_scr, h_scr, *, reverse):
    chains = [(lg, sg) for lg in range(LANE_GROUPS) for sg in range(SCAN_NSEG // SUBLANES)]

    def seg_rows(sg, j):
        return pl.ds(sg * SUBLANES * SCAN_PITCH + j, SUBLANES, stride=SCAN_PITCH)

    def step(jj, carry):
        j = (SCAN_SEG - 1 - jj) if reverse else jj
        hs, ps = carry
        new_h, new_p = [], []
        for c, (lg, sg) in enumerate(chains):
            a = a_scr[lg, seg_rows(sg, j), :]
            b = b_scr[lg, seg_rows(sg, j), :]
            h = a * hs[c] + b
            p = a * ps[c]
            h_scr[lg, seg_rows(sg, j), :] = h
            p_scr[lg, seg_rows(sg, j), :] = p
            new_h.append(h)
            new_p.append(p)
        return tuple(new_h), tuple(new_p)

    zeros = tuple(jnp.zeros((SUBLANES, LANES), _F32) for _ in chains)
    ones = tuple(jnp.ones((SUBLANES, LANES), _F32) for _ in chains)
    h_end, p_tot = lax.fori_loop(0, SCAN_SEG, step, (zeros, ones), unroll=8)

    order = range(SCAN_NSEG - 1, -1, -1) if reverse else range(SCAN_NSEG)
    carries = []
    for lg in range(LANE_GROUPS):
        carry_in = jnp.zeros((1, LANES), _F32)
        per_seg = [None] * SCAN_NSEG
        for s in order:
            c = chains.index((lg, s // SUBLANES))
            r = s % SUBLANES
            per_seg[s] = carry_in
            carry_in = p_tot[c][r:r + 1, :] * carry_in + h_end[c][r:r + 1, :]
        carries.append(per_seg)
    return carries


def _depthwise_conv(x_ref, cw, cb):
    seq = x_ref.shape[0]

    def taps(x, m2, m1, p1):
        return cb + cw[2:3, :] * x + cw[0:1, :] * m2 + cw[1:2, :] * m1 + cw[3:4, :] * p1

    row = lax.broadcasted_iota(jnp.int32, (SUBLANES, x_ref.shape[1]), 0)
    lo, hi = SUBLANES, seq - SUBLANES
    x_top = x_ref[0:lo, :]
    x_bot = x_ref[hi:seq, :]
    return jnp.concatenate([
        taps(x_top,
             jnp.where(row >= 2, pltpu.roll(x_top, 2, axis=0), 0.0),
             jnp.where(row >= 1, pltpu.roll(x_top, 1, axis=0), 0.0),
             x_ref[1:lo + 1, :]),
        taps(x_ref[lo:hi, :], x_ref[lo - 2:hi - 2, :], x_ref[lo - 1:hi - 1, :], x_ref[lo + 1:hi + 1, :]),
        taps(x_bot, x_ref[hi - 2:seq - 2, :], x_ref[hi - 1:seq - 1, :],
             jnp.where(row < SUBLANES - 1, pltpu.roll(x_bot, SUBLANES - 1, axis=0), 0.0)),
    ], axis=0)


def _lru_kernel(xr_ref, gy_ref, cw_ref, cb_ref, wg_ref, bg_ref, lam_ref, o_ref,
                a_scr, b_scr, p_scr, h_scr, sum_scr):
    xc = _depthwise_conv(xr_ref, cw_ref[...], cb_ref[...])
    xcb = xc.astype(_BF16)
    xc_half = 0.5 * xc

    for d in range(2):
        lam = lam_ref[d:d + 1, :]
        half_c = (-0.5 * LRU_C) * (jnp.maximum(-lam, 0.0) + jnp.log1p(jnp.exp(-jnp.abs(lam))))
        for half in range(D_LRU // MXU_DIM):
            cols = slice(half * MXU_DIM, (half + 1) * MXU_DIM)
            xh = xcb[:, cols]
            gr = jnp.dot(xh, wg_ref[d, 0, half], preferred_element_type=_F32)
            gi = jnp.dot(xh, wg_ref[d, 1, half], preferred_element_type=_F32)
            t_r = jnp.tanh(gr + 0.5 * bg_ref[d, 0:1, cols])
            t_i = jnp.tanh(gi + 0.5 * bg_ref[d, 1:2, cols])
            log_a = half_c[:, cols] * t_r + half_c[:, cols]
            a = jnp.exp(log_a)
            one_minus_a2 = jnp.tanh(log_a) * (-1.0 - a * a)
            mult = jnp.where(one_minus_a2 > 0.0, one_minus_a2 * lax.rsqrt(one_minus_a2), 0.0)
            b = mult * ((t_i + 1.0) * xc_half[:, cols])
            for sub in range(MXU_DIM // LANES):
                lg = half * (MXU_DIM // LANES) + sub
                lanes = slice(sub * LANES, (sub + 1) * LANES)
                for s in range(SCAN_NSEG):
                    src = slice(s * SCAN_SEG, (s + 1) * SCAN_SEG)
                    dst = pl.ds(s * SCAN_PITCH, SCAN_SEG)
                    a_scr[lg, dst, :] = a[src, lanes]
                    b_scr[lg, dst, :] = b[src, lanes]
        carries = _scan_direction(a_scr, b_scr, p_scr, h_scr, reverse=(d == 1))
        for lg in range(LANE_GROUPS):
            for s in range(SCAN_NSEG):
                rows = pl.ds(s * SCAN_PITCH, SCAN_SEG)
                dst = (slice(s * SCAN_SEG, (s + 1) * SCAN_SEG), slice(lg * LANES, (lg + 1) * LANES))
                hseg = h_scr[lg, rows, :] + p_scr[lg, rows, :] * carries[lg][s]
                if d == 0:
                    sum_scr[dst] = hseg
                else:
                    o_ref[dst] = ((sum_scr[dst] + hseg) * gy_ref[dst]).astype(o_ref.dtype)


def _recurrent_group(xr, gy, conv_w, conv_b, wg_bd, b_gate, lam, *, batch, seq):
    n = xr.shape[0]
    tok = pl.BlockSpec((seq, D_LRU), lambda b: (b, 0))
    scan_buf = pltpu.VMEM((LANE_GROUPS, SCAN_NSEG * SCAN_PITCH, LANES), _F32)
    return pl.pallas_call(
        _lru_kernel,
        grid=(batch,),
        in_specs=[
            tok, tok,
            pl.BlockSpec((CONV_WIDTH, D_LRU), lambda b: (0, 0)),
            pl.BlockSpec((1, D_LRU), lambda b: (0, 0)),
            pl.BlockSpec(wg_bd.shape, lambda b: (0, 0, 0, 0, 0)),
            pl.BlockSpec((2, 2, D_LRU), lambda b: (0, 0, 0)),
            pl.BlockSpec((2, D_LRU), lambda b: (0, 0)),
        ],
        out_specs=tok,
        out_shape=jax.ShapeDtypeStruct((n, D_LRU), _BF16),
        scratch_shapes=[scan_buf] * 4 + [pltpu.VMEM((seq, D_LRU), _F32)],
        compiler_params=_params("parallel"),
    )(xr, gy, conv_w, conv_b.reshape(1, D_LRU), wg_bd, b_gate, lam)


def _block_diag_gates(w_gate):
    per_tile = MXU_DIM // LRU_BLOCK
    w = w_gate.reshape(2, 2, N_LRU_BLOCKS // per_tile, per_tile, LRU_BLOCK, LRU_BLOCK)
    eye = jnp.eye(per_tile, dtype=w.dtype)
    bd = jnp.einsum('dghnij,nm->dghnimj', w, eye)
    return bd.reshape(2, 2, N_LRU_BLOCKS // per_tile, MXU_DIM, MXU_DIM)


def kernel(x, c, positions, w_ada, b_ada, norm_g, ffn1_w_in, ffn1_w_out, ffn2_w_in, ffn2_w_out,
           w_mix_in, w_mix_out, lambda_q, lambda_k, subln_g, conv_w, conv_b,
           lru_w_gate, lru_b_gate, lru_lambda):
    batch, seq, d = x.shape
    depth = w_ada.shape[0]
    assert (d, seq % TOKEN_TILE, seq % Q_TILE) == (D_MODEL, 0, 0)
    assert seq == SCAN_SEG * SCAN_NSEG

    ada = _ada_all_layers(c, w_ada, b_ada).reshape(depth, batch, 3 * N_SUB, d)
    cos_t, sin_t = _rope_tables(positions)
    x2 = x.reshape(batch * seq, d)
    ffn1_in, ffn1_out = ffn1_w_in[0].astype(_BF16), ffn1_w_out[0].astype(_BF16)

    for l in range(depth):
        lambda_init = 0.8 - 0.6 * math.exp(-0.3 * l)
        x2 = _ffn_sublayer(x2, ada, norm_g, ffn1_in, ffn1_out, l=l, j=0, res_w=0.5, seq=seq)
        q, k, v, xr, gy = _mix_in(x2, ada, norm_g, w_mix_in, cos_t, sin_t, l=l, seq=seq)
        later = [(ffn2_w_in, l), (ffn2_w_out, l)]
        if l + 1 < depth:
            later += [(ffn1_w_in, l + 1), (ffn1_w_out, l + 1)]
        attn, ffn2_in, ffn2_out, *next_ffn1 = _attention(
            q, k, v, lambda_q[l], lambda_k[l], subln_g[l],
            lambda_init=lambda_init, batch=batch, seq=seq, round_weights=later)
        if next_ffn1:
            ffn1_in, ffn1_out = next_ffn1
        rec = _recurrent_group(xr, gy, conv_w[l], conv_b[l],
                               (0.5 * _block_diag_gates(lru_w_gate[l])).astype(_BF16),
                               lru_b_gate[l], lru_lambda[l], batch=batch, seq=seq)
        x2 = _ffn_sublayer(x2, ada, norm_g, ffn2_in, ffn2_out, l=l, j=2, res_w=0.5, seq=seq,
                           mixer=(attn, rec, w_mix_out))
    return x2.reshape(batch, seq, d)
```

```python
import functools
import math

import jax
import jax.numpy as jnp
from jax import lax
from jax.experimental import pallas as pl
from jax.experimental.pallas import tpu as pltpu

D_MODEL = 1024
D_ATTN = 512
D_LRU = 512
HEAD_DIM = 64
V_DIM = 128
N_ATTN_HEADS = 4
N_LRU_BLOCKS = 8
LRU_BLOCK = 64
CONV_WIDTH = 4
LRU_C = 8.0
MIX_IN = 3 * D_ATTN + 2 * D_LRU
D_FF = 2816
ROPE_THETA = 10000.0
EPS = 1e-6
N_SUB = 3
LOG2_E = 1.4426950408889634

LANES = 128
SUBLANES = 8
BF16_SUBLANES = 16
MXU_DIM = 256
VMEM_LIMIT_BYTES = 56 * 1024 * 1024

TOKEN_TILE = 1024
FFN_SUB_TILE = 512
Q_TILE = 512
ATTN_HEADS_PER_STEP = 1
FF_CHUNK = MXU_DIM
ADA_COLS = 2304

SCAN_SEG = 64
SCAN_NSEG = 32
SCAN_PITCH = 68
LANE_GROUPS = D_LRU // LANES

_F32 = jnp.float32
_BF16 = jnp.bfloat16


def _params(*sem):
    return pltpu.CompilerParams(dimension_semantics=sem, vmem_limit_bytes=VMEM_LIMIT_BYTES)


def _rms(x):
    return lax.rsqrt(jnp.mean(x * x, axis=-1, keepdims=True) + EPS)


def _ada_rows(ada_ref, j):
    shift = ada_ref[0, 3 * j:3 * j + 1, :]
    scale = ada_ref[0, 3 * j + 1:3 * j + 2, :]
    gate = ada_ref[0, 3 * j + 2:3 * j + 3, :]
    return shift, scale, gate


def _prenorm(x, g_pre, scale, shift):
    return (x * _rms(x)) * (g_pre * (1.0 + scale)) + shift


def _ada_kernel(c_ref, w_ref, b_ref, o_ref):
    c = c_ref[...]
    cond = (c * jax.nn.sigmoid(c)).astype(_BF16)
    w = w_ref[0].astype(_BF16)
    o_ref[0] = jnp.dot(cond, w, preferred_element_type=_F32) + b_ref[0]


def _ada_all_layers(c, w_ada, b_ada):
    depth, d, n = w_ada.shape
    b = c.shape[0]
    return pl.pallas_call(
        _ada_kernel,
        grid=(depth, n // ADA_COLS),
        in_specs=[
            pl.BlockSpec((b, d), lambda l, i: (0, 0)),
            pl.BlockSpec((1, d, ADA_COLS), lambda l, i: (l, 0, i)),
            pl.BlockSpec((1, 1, ADA_COLS), lambda l, i: (l, 0, i)),
        ],
        out_specs=pl.BlockSpec((1, b, ADA_COLS), lambda l, i: (l, 0, i)),
        out_shape=jax.ShapeDtypeStruct((depth, b, n), _F32),
        compiler_params=_params("parallel", "parallel"),
    )(c, w_ada, b_ada.reshape(depth, 1, n))


def _rope_kernel(pos_ref, inv_ref, sign_ref, cos_ref, sin_ref):
    ang = pos_ref[...].astype(_F32) * inv_ref[...]
    cos_ref[...] = jnp.cos(ang)
    sin_ref[...] = jnp.sin(ang) * sign_ref[...]


def _rope_tables(positions):
    n = positions.size
    inv = ROPE_THETA ** (-jnp.arange(0, HEAD_DIM, 2, dtype=_F32) / HEAD_DIM)
    reps = LANES // (HEAD_DIM // 2)
    inv_row = jnp.tile(inv, reps).reshape(1, LANES)
    half = jnp.concatenate([-jnp.ones((HEAD_DIM // 2,), _F32), jnp.ones((HEAD_DIM // 2,), _F32)])
    sign_row = jnp.tile(half, LANES // HEAD_DIM).reshape(1, LANES)
    row = pl.BlockSpec((1, LANES), lambda i: (0, 0))
    tab = pl.BlockSpec((TOKEN_TILE, LANES), lambda i: (i, 0))
    return pl.pallas_call(
        _rope_kernel,
        grid=(n // TOKEN_TILE,),
        in_specs=[pl.BlockSpec((TOKEN_TILE, 1), lambda i: (i, 0)), row, row],
        out_specs=[tab, tab],
        out_shape=[jax.ShapeDtypeStruct((n, LANES), _F32)] * 2,
        compiler_params=_params("parallel"),
    )(positions.reshape(n, 1), inv_row, sign_row)


def _ffn_kernel(*refs, j, res_w, after_mixer):
    if after_mixer:
        x_ref, attn_ref, rec_ref, wmix_ref, ada_ref, g_ref, win_ref, wout_ref, o_ref, act_ref = refs
    else:
        x_ref, ada_ref, g_ref, win_ref, wout_ref, o_ref, act_ref = refs
    shift, scale, gate = _ada_rows(ada_ref, j)
    halves = [slice(r * FFN_SUB_TILE, (r + 1) * FFN_SUB_TILE) for r in range(TOKEN_TILE // FFN_SUB_TILE)]

    xs, hs = [], []
    for rows in halves:
        x = x_ref[rows, :]
        if after_mixer:
            _, _, mix_gate = _ada_rows(ada_ref, 1)
            m = (jnp.dot(attn_ref[rows, :], wmix_ref[:D_ATTN, :].astype(_BF16), preferred_element_type=_F32)
                 + jnp.dot(rec_ref[rows, :], wmix_ref[D_ATTN:, :].astype(_BF16), preferred_element_type=_F32))
            x = x + mix_gate * ((m * _rms(m)) * g_ref[3:4, :])
        xs.append(x)
        hs.append(_prenorm(x, g_ref[2 * j:2 * j + 1, :], scale, shift).astype(_BF16))

    def finish(rows, x, y):
        y = (y * _rms(y)) * g_ref[2 * j + 1:2 * j + 2, :]
        o_ref[rows, :] = x + (res_w * gate) * y

    pending = None
    for rows, x, h in zip(halves, xs, hs):
        for c in range(D_FF // FF_CHUNK):
            lo = c * FF_CHUNK
            g = jnp.dot(h, win_ref[:, lo:lo + FF_CHUNK], preferred_element_type=_F32)
            u = jnp.dot(h, win_ref[:, D_FF + lo:D_FF + lo + FF_CHUNK], preferred_element_type=_F32)
            act_ref[rows, lo:lo + FF_CHUNK] = (g * jax.nn.sigmoid(g) * u).astype(_BF16)
        if pending is not None:
            finish(*pending)
        pending = (rows, x, jnp.dot(act_ref[rows, :], wout_ref[...], preferred_element_type=_F32))
    finish(*pending)


def _layer_block(shape, l, **kw):
    zeros = (0,) * len(shape)
    return pl.BlockSpec((None,) + tuple(shape), lambda i: (l,) + zeros, **kw)


def _ada_block(l, tiles_per_seq):
    return pl.BlockSpec((None, 1, 3 * N_SUB, D_MODEL), lambda i: (l, i // tiles_per_seq, 0, 0))


def _ffn_sublayer(x2, ada, norm_g, w_in, w_out, *, l, j, res_w, seq, mixer=None):
    n, d = x2.shape
    tiles_per_seq = seq // TOKEN_TILE
    resident = dict(pipeline_mode=pl.Buffered(1))
    tok = pl.BlockSpec((TOKEN_TILE, d), lambda i: (i, 0))
    in_specs, args = [tok], [x2]
    if mixer is not None:
        col = pl.BlockSpec((TOKEN_TILE, D_ATTN), lambda i: (i, 0))
        in_specs += [col, col, _layer_block((d, d), l, **resident)]
        args += list(mixer)
    in_specs += [
        _ada_block(l, tiles_per_seq),
        _layer_block((2 * N_SUB, d), l),
        pl.BlockSpec((d, 2 * D_FF), lambda i: (0, 0), **resident),
        pl.BlockSpec((D_FF, d), lambda i: (0, 0), **resident),
    ]
    args += [ada, norm_g, w_in, w_out]
    return pl.pallas_call(
        functools.partial(_ffn_kernel, j=j, res_w=res_w, after_mixer=mixer is not None),
        grid=(n // TOKEN_TILE,),
        in_specs=in_specs,
        out_specs=tok,
        out_shape=jax.ShapeDtypeStruct((n, d), _F32),
        scratch_shapes=[pltpu.VMEM((TOKEN_TILE, D_FF), _BF16)],
        compiler_params=_params("parallel"),
    )(*args)


def _rope(t, cos, sin_signed):
    lane = lax.broadcasted_iota(jnp.int32, t.shape, 1)
    first_half = (lane % HEAD_DIM) < (HEAD_DIM // 2)
    partner = jnp.where(first_half,
                        pltpu.roll(t, LANES - HEAD_DIM // 2, axis=1),
                        pltpu.roll(t, HEAD_DIM // 2, axis=1))
    return t * cos + partner * sin_signed


def _mix_in_kernel(x_ref, ada_ref, g_ref, w_ref, cos_ref, sin_ref,
                   q_ref, k_ref, v_ref, xr_ref, gy_ref):
    j = 1
    x = x_ref[...]
    shift, scale, _ = _ada_rows(ada_ref, j)
    h = _prenorm(x, g_ref[2 * j:2 * j + 1, :], scale, shift).astype(_BF16)
    cos = cos_ref[...]
    sin = sin_ref[...]
    q_scale = HEAD_DIM ** -0.5 * LOG2_E

    def proj(lo, width):
        return jnp.dot(h, w_ref[:, lo:lo + width].astype(_BF16), preferred_element_type=_F32)

    q = proj(0, D_ATTN)
    k = proj(D_ATTN, D_ATTN)
    for hd in range(N_ATTN_HEADS):
        head = slice(hd * LANES, (hd + 1) * LANES)
        q_ref[:, head] = (_rope(q[:, head], cos, sin) * q_scale).astype(_BF16)
        k_ref[:, head] = _rope(k[:, head], cos, sin).astype(_BF16)
    v_ref[...] = proj(2 * D_ATTN, D_ATTN).astype(_BF16)
    xr_ref[...] = proj(3 * D_ATTN, D_LRU)
    gy_ref[...] = jax.nn.gelu(proj(3 * D_ATTN + D_LRU, D_LRU))


def _mix_in(x2, ada, norm_g, w_in, cos_t, sin_t, *, l, seq):
    n, d = x2.shape
    tiles_per_seq = seq // TOKEN_TILE
    half = lambda dt: jax.ShapeDtypeStruct((n, D_ATTN), dt)
    col = pl.BlockSpec((TOKEN_TILE, D_ATTN), lambda i: (i, 0))
    tab = pl.BlockSpec((TOKEN_TILE, LANES), lambda i: (i, 0))
    return pl.pallas_call(
        _mix_in_kernel,
        grid=(n // TOKEN_TILE,),
        in_specs=[
            pl.BlockSpec((TOKEN_TILE, d), lambda i: (i, 0)),
            _ada_block(l, tiles_per_seq),
            _layer_block((2 * N_SUB, d), l),
            _layer_block((d, MIX_IN), l, pipeline_mode=pl.Buffered(1)),
            tab, tab,
        ],
        out_specs=[col] * 5,
        out_shape=[half(_BF16), half(_BF16), half(_BF16), half(_F32), half(_F32)],
        compiler_params=_params("parallel"),
    )(x2, ada, norm_g, w_in, cos_t, sin_t)


def _store_scores(q, k, s_ref, m_ref):
    lane = lax.broadcasted_iota(jnp.int32, q.shape, 1)
    first_map = lane < HEAD_DIM
    zero = jnp.zeros_like(q)
    nt = (((1,), (1,)), ((), ()))
    for m, qm in enumerate((jnp.where(first_map, q, zero), jnp.where(first_map, zero, q))):
        s = lax.dot_general(qm, k, nt, preferred_element_type=_F32)
        s_ref[m] = s
        m_ref[m] = jnp.max(s, axis=-1, keepdims=True)


def _diff_softmax_pv(s_ref, m_ref, v_ones, lam, out_gain):
    maps = []
    for m in range(2):
        p = jnp.exp2(s_ref[m] - m_ref[m])
        pv = jnp.dot(p.astype(_BF16), v_ones, preferred_element_type=_F32)
        maps.append(pv[:, :V_DIM] * (1.0 / pv[:, V_DIM:]))
    o = maps[0] - lam * maps[1]
    return (o * _rms(o)) * out_gain


def _attn_kernel(*refs, lambda_init, n_round):
    q_ref, qn_ref, k_ref, kn_ref, v_ref, lq_ref, lk_ref, sg_ref = refs[:8]
    f32_weights = refs[8:8 + n_round]
    o_ref = refs[8 + n_round]
    bf16_weights = refs[9 + n_round:9 + 2 * n_round]
    s0_ref, s1_ref, m0_ref, m1_ref = refs[9 + 2 * n_round:]

    for src, dst in zip(f32_weights, bf16_weights):
        dst[...] = src[...].astype(dst.dtype)

    q_tiles = q_ref.shape[0] // Q_TILE
    scores = ((s0_ref, m0_ref), (s1_ref, m1_ref))
    tiles = [(slice(i * Q_TILE, (i + 1) * Q_TILE), slice(hd * V_DIM, (hd + 1) * V_DIM))
             for hd in range(ATTN_HEADS_PER_STEP) for i in range(q_tiles)]

    @pl.when(pl.program_id(0) == 0)
    def _():
        rows, head = tiles[0]
        _store_scores(q_ref[rows, head], k_ref[:, head], *scores[0])

    lqk = lq_ref[...] * lk_ref[...]
    lam = (jnp.exp(jnp.sum(lqk[0:1, :], axis=-1, keepdims=True))
           - jnp.exp(jnp.sum(lqk[1:2, :], axis=-1, keepdims=True)) + lambda_init)
    out_gain = sg_ref[...] * (1.0 - lambda_init)

    for i, (rows, head) in enumerate(tiles):
        if i + 1 < len(tiles):
            next_rows, next_head = tiles[i + 1]
            _store_scores(q_ref[next_rows, next_head], k_ref[:, next_head], *scores[(i + 1) % 2])
        else:
            _store_scores(qn_ref[...], kn_ref[...], *scores[(i + 1) % 2])
        v_ones = jnp.concatenate([v_ref[:, head], jnp.ones((v_ref.shape[0], V_DIM), _BF16)], axis=1)
        o_ref[rows, head] = _diff_softmax_pv(*scores[i % 2], v_ones, lam, out_gain).astype(o_ref.dtype)


def _row_chunks(rows, steps):
    chunks = steps
    while rows % chunks or (rows // chunks) % BF16_SUBLANES:
        chunks //= 2
    return chunks, steps // chunks


def _attention(q, k, v, lq, lk, sg, *, lambda_init, batch, seq, round_weights=()):
    n = q.shape[0]
    q_tiles = seq // Q_TILE
    assert (q_tiles * ATTN_HEADS_PER_STEP) % 2 == 0
    groups = N_ATTN_HEADS // ATTN_HEADS_PER_STEP
    steps = batch * groups

    def group_block(t):
        return t // groups, t % groups

    def next_step(t):
        return jnp.minimum(t + 1, steps - 1)

    def next_first_tile(t):
        b, g = group_block(next_step(t))
        return b * q_tiles, g * ATTN_HEADS_PER_STEP

    def next_first_head(t):
        b, g = group_block(next_step(t))
        return b, g * ATTN_HEADS_PER_STEP

    whole = pl.BlockSpec((seq, ATTN_HEADS_PER_STEP * V_DIM), group_block)
    small = lambda shape: pl.BlockSpec(shape, lambda t: (0, 0))
    scores = pltpu.VMEM((2, Q_TILE, seq), _F32)
    row_max = pltpu.VMEM((2, Q_TILE, 1), _F32)

    round_in, round_out, round_shapes = [], [], []
    for w, l in round_weights:
        rows, cols = w.shape[1:]
        chunks, rep = _row_chunks(rows, steps)
        round_in.append(pl.BlockSpec((None, rows // chunks, cols), lambda t, l=l, rep=rep: (l, t // rep, 0)))
        round_out.append(pl.BlockSpec((rows // chunks, cols), lambda t, rep=rep: (t // rep, 0)))
        round_shapes.append(jax.ShapeDtypeStruct((rows, cols), _BF16))

    return pl.pallas_call(
        functools.partial(_attn_kernel, lambda_init=lambda_init, n_round=len(round_weights)),
        grid=(steps,),
        in_specs=[
            whole,
            pl.BlockSpec((Q_TILE, V_DIM), next_first_tile),
            whole,
            pl.BlockSpec((seq, V_DIM), next_first_head),
            whole,
            small((2, HEAD_DIM)), small((2, HEAD_DIM)), small((1, V_DIM)),
        ] + round_in,
        out_specs=[whole] + round_out,
        out_shape=[jax.ShapeDtypeStruct((n, D_ATTN), _BF16)] + round_shapes,
        scratch_shapes=[scores, scores, row_max, row_max],
        compiler_params=_params("arbitrary"),
    )(q, q, k, k, v, lq, lk, sg.reshape(1, V_DIM), *[w for w, _ in round_weights])


def _scan_direction(a_scr, b_scr, p_scr, h_scr, *, reverse):
    chains = [(lg, sg) for lg in range(LANE_GROUPS) for sg in range(SCAN_NSEG // SUBLANES)]

    def seg_rows(sg, j):
        return pl.ds(sg * SUBLANES * SCAN_PITCH + j, SUBLANES, stride=SCAN_PITCH)

    def step(jj, carry):
        j = (SCAN_SEG - 1 - jj) if reverse else jj
        hs, ps = carry
        new_h, new_p = [], []
        for c, (lg, sg) in enumerate(chains):
            a = a_scr[lg, seg_rows(sg, j), :]
            b = b_scr[lg, seg_rows(sg, j), :]
            h = a * hs[c] + b
            p = a * ps[c]
            h_scr[lg, seg_rows(sg, j), :] = h
            p_scr[lg, seg_rows(sg, j), :] = p
            new_h.append(h)
            new_p.append(p)
        return tuple(new_h), tuple(new_p)

    zeros = tuple(jnp.zeros((SUBLANES, LANES), _F32) for _ in chains)
    ones = tuple(jnp.ones((SUBLANES, LANES), _F32) for _ in chains)
    h_end, p_tot = lax.fori_loop(0, SCAN_SEG, step, (zeros, ones), unroll=8)

    order = range(SCAN_NSEG - 1, -1, -1) if reverse else range(SCAN_NSEG)
    carries = []
    for lg in range(LANE_GROUPS):
        carry_in = jnp.zeros((1, LANES), _F32)
        per_seg = [None] * SCAN_NSEG
        for s in order:
            c = chains.index((lg, s // SUBLANES))
            r = s % SUBLANES
            per_seg[s] = carry_in
            carry_in = p_tot[c][r:r + 1, :] * carry_in + h_end[c][r:r + 1, :]
        carries.append(per_seg)
    return carries


def _depthwise_conv(x_ref, cw, cb):
    seq = x_ref.shape[0]

    def taps(x, m2, m1, p1):
        return cb + cw[2:3, :] * x + cw[0:1, :] * m2 + cw[1:2, :] * m1 + cw[3:4, :] * p1

    row = lax.broadcasted_iota(jnp.int32, (SUBLANES, x_ref.shape[1]), 0)
    lo, hi = SUBLANES, seq - SUBLANES
    x_top = x_ref[0:lo, :]
    x_bot = x_ref[hi:seq, :]
    return jnp.concatenate([
        taps(x_top,
             jnp.where(row >= 2, pltpu.roll(x_top, 2, axis=0), 0.0),
             jnp.where(row >= 1, pltpu.roll(x_top, 1, axis=0), 0.0),
             x_ref[1:lo + 1, :]),
        taps(x_ref[lo:hi, :], x_ref[lo - 2:hi - 2, :], x_ref[lo - 1:hi - 1, :], x_ref[lo + 1:hi + 1, :]),
        taps(x_bot, x_ref[hi - 2:seq - 2, :], x_ref[hi - 1:seq - 1, :],
             jnp.where(row < SUBLANES - 1, pltpu.roll(x_bot, SUBLANES - 1, axis=0), 0.0)),
    ], axis=0)


def _lru_kernel(xr_ref, gy_ref, cw_ref, cb_ref, wg_ref, bg_ref, lam_ref, o_ref,
                a_scr, b_scr, p_scr, h_scr, sum_scr):
    xc = _depthwise_conv(xr_ref, cw_ref[...], cb_ref[...])
    xcb = xc.astype(_BF16)
    xc_half = 0.5 * xc

    for d in range(2):
        lam = lam_ref[d:d + 1, :]
        half_c = (-0.5 * LRU_C) * (jnp.maximum(-lam, 0.0) + jnp.log1p(jnp.exp(-jnp.abs(lam))))
        for half in range(D_LRU // MXU_DIM):
            cols = slice(half * MXU_DIM, (half + 1) * MXU_DIM)
            xh = xcb[:, cols]
            gr = jnp.dot(xh, wg_ref[d, 0, half], preferred_element_type=_F32)
            gi = jnp.dot(xh, wg_ref[d, 1, half], preferred_element_type=_F32)
            t_r = jnp.tanh(gr + 0.5 * bg_ref[d, 0:1, cols])
            t_i = jnp.tanh(gi + 0.5 * bg_ref[d, 1:2, cols])
            log_a = half_c[:, cols] * t_r + half_c[:, cols]
            a = jnp.exp(log_a)
            one_minus_a2 = jnp.tanh(log_a) * (-1.0 - a * a)
            mult = jnp.where(one_minus_a2 > 0.0, one_minus_a2 * lax.rsqrt(one_minus_a2), 0.0)
            b = mult * ((t_i + 1.0) * xc_half[:, cols])
            for sub in range(MXU_DIM // LANES):
                lg = half * (MXU_DIM // LANES) + sub
                lanes = slice(sub * LANES, (sub + 1) * LANES)
                for s in range(SCAN_NSEG):
                    src = slice(s * SCAN_SEG, (s + 1) * SCAN_SEG)
                    dst = pl.ds(s * SCAN_PITCH, SCAN_SEG)
                    a_scr[lg, dst, :] = a[src, lanes]
                    b_scr[lg, dst, :] = b[src, lanes]
        carries = _scan_direction(a_scr, b_scr, p_scr, h_scr, reverse=(d == 1))
        for lg in range(LANE_GROUPS):
            for s in range(SCAN_NSEG):
                rows = pl.ds(s * SCAN_PITCH, SCAN_SEG)
                dst = (slice(s * SCAN_SEG, (s + 1) * SCAN_SEG), slice(lg * LANES, (lg + 1) * LANES))
                hseg = h_scr[lg, rows, :] + p_scr[lg, rows, :] * carries[lg][s]
                if d == 0:
                    sum_scr[dst] = hseg
                else:
                    o_ref[dst] = ((sum_scr[dst] + hseg) * gy_ref[dst]).astype(o_ref.dtype)


def _recurrent_group(xr, gy, conv_w, conv_b, wg_bd, b_gate, lam, *, batch, seq):
    n = xr.shape[0]
    tok = pl.BlockSpec((seq, D_LRU), lambda b: (b, 0))
    scan_buf = pltpu.VMEM((LANE_GROUPS, SCAN_NSEG * SCAN_PITCH, LANES), _F32)
    return pl.pallas_call(
        _lru_kernel,
        grid=(batch,),
        in_specs=[
            tok, tok,
            pl.BlockSpec((CONV_WIDTH, D_LRU), lambda b: (0, 0)),
            pl.BlockSpec((1, D_LRU), lambda b: (0, 0)),
            pl.BlockSpec(wg_bd.shape, lambda b: (0, 0, 0, 0, 0)),
            pl.BlockSpec((2, 2, D_LRU), lambda b: (0, 0, 0)),
            pl.BlockSpec((2, D_LRU), lambda b: (0, 0)),
        ],
        out_specs=tok,
        out_shape=jax.ShapeDtypeStruct((n, D_LRU), _BF16),
        scratch_shapes=[scan_buf] * 4 + [pltpu.VMEM((seq, D_LRU), _F32)],
        compiler_params=_params("parallel"),
    )(xr, gy, conv_w, conv_b.reshape(1, D_LRU), wg_bd, b_gate, lam)


def _block_diag_gates(w_gate):
    per_tile = MXU_DIM // LRU_BLOCK
    w = w_gate.reshape(2, 2, N_LRU_BLOCKS // per_tile, per_tile, LRU_BLOCK, LRU_BLOCK)
    eye = jnp.eye(per_tile, dtype=w.dtype)
    bd = jnp.einsum('dghnij,nm->dghnimj', w, eye)
    return bd.reshape(2, 2, N_LRU_BLOCKS // per_tile, MXU_DIM, MXU_DIM)


def kernel(x, c, positions, w_ada, b_ada, norm_g, ffn1_w_in, ffn1_w_out, ffn2_w_in, ffn2_w_out,
           w_mix_in, w_mix_out, lambda_q, lambda_k, subln_g, conv_w, conv_b,
           lru_w_gate, lru_b_gate, lru_lambda):
    batch, seq, d = x.shape
    depth = w_ada.shape[0]
    assert (d, seq % TOKEN_TILE, seq % Q_TILE) == (D_MODEL, 0, 0)
    assert seq == SCAN_SEG * SCAN_NSEG

    ada = _ada_all_layers(c, w_ada, b_ada).reshape(depth, batch, 3 * N_SUB, d)
    cos_t, sin_t = _rope_tables(positions)
    x2 = x.reshape(batch * seq, d)
    ffn1_in, ffn1_out = ffn1_w_in[0].astype(_BF16), ffn1_w_out[0].astype(_BF16)

    for l in range(depth):
        lambda_init = 0.8 - 0.6 * math.exp(-0.3 * l)
        x2 = _ffn_sublayer(x2, ada, norm_g, ffn1_in, ffn1_out, l=l, j=0, res_w=0.5, seq=seq)
        q, k, v, xr, gy = _mix_in(x2, ada, norm_g, w_mix_in, cos_t, sin_t, l=l, seq=seq)
        later = [(ffn2_w_in, l), (ffn2_w_out, l)]
        if l + 1 < depth:
            later += [(ffn1_w_in, l + 1), (ffn1_w_out, l + 1)]
        attn, ffn2_in, ffn2_out, *next_ffn1 = _attention(
            q, k, v, lambda_q[l], lambda_k[l], subln_g[l],
            lambda_init=lambda_init, batch=batch, seq=seq, round_weights=later)
        if next_ffn1:
            ffn1_in, ffn1_out = next_ffn1
        rec = _recurrent_group(xr, gy, conv_w[l], conv_b[l],
                               (0.5 * _block_diag_gates(lru_w_gate[l])).astype(_BF16),
                               lru_b_gate[l], lru_lambda[l], batch=batch, seq=seq)
        x2 = _ffn_sublayer(x2, ada, norm_g, ffn2_in, ffn2_out, l=l, j=2, res_w=0.5, seq=seq,
                           mixer=(attn, rec, w_mix_out))
    return x2.reshape(batch, seq, d)
```

```python
import functools
import math

import jax
import jax.numpy as jnp
from jax import lax
from jax.experimental import pallas as pl
from jax.experimental.pallas import tpu as pltpu

D_MODEL = 1024
D_ATTN = 512
D_LRU = 512
HEAD_DIM = 64
V_DIM = 128
N_ATTN_HEADS = 4
N_LRU_BLOCKS = 8
LRU_BLOCK = 64
CONV_WIDTH = 4
LRU_C = 8.0
MIX_IN = 3 * D_ATTN + 2 * D_LRU
D_FF = 2816
ROPE_THETA = 10000.0
EPS = 1e-6
N_SUB = 3
LOG2_E = 1.4426950408889634

LANES = 128
SUBLANES = 8
BF16_SUBLANES = 16
MXU_DIM = 256
VMEM_LIMIT_BYTES = 56 * 1024 * 1024

TOKEN_TILE = 1024
FFN_SUB_TILE = 256
Q_TILE = 512
ATTN_HEADS_PER_STEP = 1
FF_CHUNK = MXU_DIM
ADA_COLS = 2304

SCAN_SEG = 64
SCAN_NSEG = 32
SCAN_PITCH = 68
LANE_GROUPS = D_LRU // LANES

_F32 = jnp.float32
_BF16 = jnp.bfloat16


def _params(*sem):
    return pltpu.CompilerParams(dimension_semantics=sem, vmem_limit_bytes=VMEM_LIMIT_BYTES)


def _rms(x):
    return lax.rsqrt(jnp.mean(x * x, axis=-1, keepdims=True) + EPS)


def _ada_rows(ada_ref, j):
    shift = ada_ref[0, 3 * j:3 * j + 1, :]
    scale = ada_ref[0, 3 * j + 1:3 * j + 2, :]
    gate = ada_ref[0, 3 * j + 2:3 * j + 3, :]
    return shift, scale, gate


def _prenorm(x, g_pre, scale, shift):
    return (x * _rms(x)) * (g_pre * (1.0 + scale)) + shift


def _ada_kernel(c_ref, w_ref, b_ref, o_ref):
    c = c_ref[...]
    cond = (c * jax.nn.sigmoid(c)).astype(_BF16)
    w = w_ref[0].astype(_BF16)
    o_ref[0] = jnp.dot(cond, w, preferred_element_type=_F32) + b_ref[0]


def _ada_all_layers(c, w_ada, b_ada):
    depth, d, n = w_ada.shape
    b = c.shape[0]
    return pl.pallas_call(
        _ada_kernel,
        grid=(depth, n // ADA_COLS),
        in_specs=[
            pl.BlockSpec((b, d), lambda l, i: (0, 0)),
            pl.BlockSpec((1, d, ADA_COLS), lambda l, i: (l, 0, i)),
            pl.BlockSpec((1, 1, ADA_COLS), lambda l, i: (l, 0, i)),
        ],
        out_specs=pl.BlockSpec((1, b, ADA_COLS), lambda l, i: (l, 0, i)),
        out_shape=jax.ShapeDtypeStruct((depth, b, n), _F32),
        compiler_params=_params("parallel", "parallel"),
    )(c, w_ada, b_ada.reshape(depth, 1, n))


def _rope_kernel(pos_ref, inv_ref, sign_ref, cos_ref, sin_ref):
    ang = pos_ref[...].astype(_F32) * inv_ref[...]
    cos_ref[...] = jnp.cos(ang)
    sin_ref[...] = jnp.sin(ang) * sign_ref[...]


def _rope_tables(positions):
    n = positions.size
    inv = ROPE_THETA ** (-jnp.arange(0, HEAD_DIM, 2, dtype=_F32) / HEAD_DIM)
    reps = LANES // (HEAD_DIM // 2)
    inv_row = jnp.tile(inv, reps).reshape(1, LANES)
    half = jnp.concatenate([-jnp.ones((HEAD_DIM // 2,), _F32), jnp.ones((HEAD_DIM // 2,), _F32)])
    sign_row = jnp.tile(half, LANES // HEAD_DIM).reshape(1, LANES)
    row = pl.BlockSpec((1, LANES), lambda i: (0, 0))
    tab = pl.BlockSpec((TOKEN_TILE, LANES), lambda i: (i, 0))
    return pl.pallas_call(
        _rope_kernel,
        grid=(n // TOKEN_TILE,),
        in_specs=[pl.BlockSpec((TOKEN_TILE, 1), lambda i: (i, 0)), row, row],
        out_specs=[tab, tab],
        out_shape=[jax.ShapeDtypeStruct((n, LANES), _F32)] * 2,
        compiler_params=_params("parallel"),
    )(positions.reshape(n, 1), inv_row, sign_row)


def _round_specs(round_weights, steps):
    ins, outs, shapes = [], [], []
    for w, l in round_weights:
        rows, cols = w.shape[1:]
        chunks = steps
        while rows % chunks or (rows // chunks) % BF16_SUBLANES:
            chunks //= 2
        rep = steps // chunks
        ins.append(pl.BlockSpec((None, rows // chunks, cols), lambda t, l=l, rep=rep: (l, t // rep, 0)))
        outs.append(pl.BlockSpec((rows // chunks, cols), lambda t, rep=rep: (t // rep, 0)))
        shapes.append(jax.ShapeDtypeStruct((rows, cols), _BF16))
    return ins, outs, shapes


def _round_blocks(f32_refs, bf16_refs):
    for src, dst in zip(f32_refs, bf16_refs):
        dst[...] = src[...].astype(dst.dtype)


def _ffn_kernel(*refs, j, res_w, after_mixer, n_round):
    n_in = 8 if after_mixer else 5
    if after_mixer:
        x_ref, attn_ref, rec_ref, wmix_ref, ada_ref, g_ref, win_ref, wout_ref = refs[:n_in]
    else:
        x_ref, ada_ref, g_ref, win_ref, wout_ref = refs[:n_in]
    o_ref, act_ref = refs[n_in + n_round], refs[-1]
    _round_blocks(refs[n_in:n_in + n_round], refs[n_in + n_round + 1:n_in + 2 * n_round + 1])
    shift, scale, gate = _ada_rows(ada_ref, j)
    halves = [slice(r * FFN_SUB_TILE, (r + 1) * FFN_SUB_TILE) for r in range(TOKEN_TILE // FFN_SUB_TILE)]

    xs, hs = [], []
    for rows in halves:
        x = x_ref[rows, :]
        if after_mixer:
            _, _, mix_gate = _ada_rows(ada_ref, 1)
            m = (jnp.dot(attn_ref[rows, :], wmix_ref[:D_ATTN, :].astype(_BF16), preferred_element_type=_F32)
                 + jnp.dot(rec_ref[rows, :], wmix_ref[D_ATTN:, :].astype(_BF16), preferred_element_type=_F32))
            x = x + mix_gate * ((m * _rms(m)) * g_ref[3:4, :])
        xs.append(x)
        hs.append(_prenorm(x, g_ref[2 * j:2 * j + 1, :], scale, shift).astype(_BF16))

    def finish(rows, x, y):
        y = (y * _rms(y)) * g_ref[2 * j + 1:2 * j + 2, :]
        o_ref[rows, :] = x + (res_w * gate) * y

    pending = None
    for rows, x, h in zip(halves, xs, hs):
        for c in range(D_FF // FF_CHUNK):
            lo = c * FF_CHUNK
            g = jnp.dot(h, win_ref[:, lo:lo + FF_CHUNK], preferred_element_type=_F32)
            u = jnp.dot(h, win_ref[:, D_FF + lo:D_FF + lo + FF_CHUNK], preferred_element_type=_F32)
            act_ref[rows, lo:lo + FF_CHUNK] = (g * jax.nn.sigmoid(g) * u).astype(_BF16)
        if pending is not None:
            finish(*pending)
        pending = (rows, x, jnp.dot(act_ref[rows, :], wout_ref[...], preferred_element_type=_F32))
    finish(*pending)


def _layer_block(shape, l, **kw):
    zeros = (0,) * len(shape)
    return pl.BlockSpec((None,) + tuple(shape), lambda i: (l,) + zeros, **kw)


def _ada_block(l, tiles_per_seq):
    return pl.BlockSpec((None, 1, 3 * N_SUB, D_MODEL), lambda i: (l, i // tiles_per_seq, 0, 0))


def _ffn_sublayer(x2, ada, norm_g, w_in, w_out, *, l, j, res_w, seq, mixer=None, round_weights=()):
    n, d = x2.shape
    tiles_per_seq = seq // TOKEN_TILE
    resident = dict(pipeline_mode=pl.Buffered(1))
    tok = pl.BlockSpec((TOKEN_TILE, d), lambda i: (i, 0))
    in_specs, args = [tok], [x2]
    if mixer is not None:
        col = pl.BlockSpec((TOKEN_TILE, D_ATTN), lambda i: (i, 0))
        in_specs += [col, col, _layer_block((d, d), l, **resident)]
        args += list(mixer)
    in_specs += [
        _ada_block(l, tiles_per_seq),
        _layer_block((2 * N_SUB, d), l),
        pl.BlockSpec((d, 2 * D_FF), lambda i: (0, 0), **resident),
        pl.BlockSpec((D_FF, d), lambda i: (0, 0), **resident),
    ]
    args += [ada, norm_g, w_in, w_out]
    round_in, round_out, round_shapes = _round_specs(round_weights, n // TOKEN_TILE)
    out = pl.pallas_call(
        functools.partial(_ffn_kernel, j=j, res_w=res_w, after_mixer=mixer is not None,
                          n_round=len(round_weights)),
        grid=(n // TOKEN_TILE,),
        in_specs=in_specs + round_in,
        out_specs=[tok] + round_out,
        out_shape=[jax.ShapeDtypeStruct((n, d), _F32)] + round_shapes,
        scratch_shapes=[pltpu.VMEM((TOKEN_TILE, D_FF), _BF16)],
        compiler_params=_params("arbitrary" if round_weights else "parallel"),
    )(*args, *[w for w, _ in round_weights])
    return out if round_weights else out[0]


def _rope(t, cos, sin_signed):
    lane = lax.broadcasted_iota(jnp.int32, t.shape, 1)
    first_half = (lane % HEAD_DIM) < (HEAD_DIM // 2)
    partner = jnp.where(first_half,
                        pltpu.roll(t, LANES - HEAD_DIM // 2, axis=1),
                        pltpu.roll(t, HEAD_DIM // 2, axis=1))
    return t * cos + partner * sin_signed


def _mix_in_kernel(x_ref, ada_ref, g_ref, w_ref, cos_ref, sin_ref,
                   q_ref, k_ref, v_ref, xr_ref, gy_ref):
    j = 1
    x = x_ref[...]
    shift, scale, _ = _ada_rows(ada_ref, j)
    h = _prenorm(x, g_ref[2 * j:2 * j + 1, :], scale, shift).astype(_BF16)
    cos = cos_ref[...]
    sin = sin_ref[...]
    q_scale = HEAD_DIM ** -0.5 * LOG2_E

    def proj(lo, width):
        return jnp.dot(h, w_ref[:, lo:lo + width].astype(_BF16), preferred_element_type=_F32)

    q = proj(0, D_ATTN)
    k = proj(D_ATTN, D_ATTN)
    for hd in range(N_ATTN_HEADS):
        head = slice(hd * LANES, (hd + 1) * LANES)
        q_ref[:, head] = (_rope(q[:, head], cos, sin) * q_scale).astype(_BF16)
        k_ref[:, head] = _rope(k[:, head], cos, sin).astype(_BF16)
    v_ref[...] = proj(2 * D_ATTN, D_ATTN).astype(_BF16)
    xr_ref[...] = proj(3 * D_ATTN, D_LRU)
    gy_ref[...] = jax.nn.gelu(proj(3 * D_ATTN + D_LRU, D_LRU))


def _mix_in(x2, ada, norm_g, w_in, cos_t, sin_t, *, l, seq):
    n, d = x2.shape
    tiles_per_seq = seq // TOKEN_TILE
    half = lambda dt: jax.ShapeDtypeStruct((n, D_ATTN), dt)
    col = pl.BlockSpec((TOKEN_TILE, D_ATTN), lambda i: (i, 0))
    tab = pl.BlockSpec((TOKEN_TILE, LANES), lambda i: (i, 0))
    return pl.pallas_call(
        _mix_in_kernel,
        grid=(n // TOKEN_TILE,),
        in_specs=[
            pl.BlockSpec((TOKEN_TILE, d), lambda i: (i, 0)),
            _ada_block(l, tiles_per_seq),
            _layer_block((2 * N_SUB, d), l),
            _layer_block((d, MIX_IN), l, pipeline_mode=pl.Buffered(1)),
            tab, tab,
        ],
        out_specs=[col] * 5,
        out_shape=[half(_BF16), half(_BF16), half(_BF16), half(_F32), half(_F32)],
        compiler_params=_params("parallel"),
    )(x2, ada, norm_g, w_in, cos_t, sin_t)


def _store_scores(q, k, s_ref, m_ref):
    lane = lax.broadcasted_iota(jnp.int32, q.shape, 1)
    first_map = lane < HEAD_DIM
    zero = jnp.zeros_like(q)
    nt = (((1,), (1,)), ((), ()))
    for m, qm in enumerate((jnp.where(first_map, q, zero), jnp.where(first_map, zero, q))):
        s = lax.dot_general(qm, k, nt, preferred_element_type=_F32)
        s_ref[m] = s
        m_ref[m] = jnp.max(s, axis=-1, keepdims=True)


def _diff_softmax_pv(s_ref, m_ref, v_ones, lam, out_gain):
    maps = []
    for m in range(2):
        p = jnp.exp2(s_ref[m] - m_ref[m])
        pv = jnp.dot(p.astype(_BF16), v_ones, preferred_element_type=_F32)
        maps.append(pv[:, :V_DIM] * (1.0 / pv[:, V_DIM:]))
    o = maps[0] - lam * maps[1]
    return (o * _rms(o)) * out_gain


def _attn_kernel(*refs, lambda_init, n_round):
    q_ref, qn_ref, k_ref, kn_ref, v_ref, lq_ref, lk_ref, sg_ref = refs[:8]
    f32_weights = refs[8:8 + n_round]
    o_ref = refs[8 + n_round]
    bf16_weights = refs[9 + n_round:9 + 2 * n_round]
    s0_ref, s1_ref, m0_ref, m1_ref = refs[9 + 2 * n_round:]

    _round_blocks(f32_weights, bf16_weights)

    q_tiles = q_ref.shape[0] // Q_TILE
    scores = ((s0_ref, m0_ref), (s1_ref, m1_ref))
    tiles = [(slice(i * Q_TILE, (i + 1) * Q_TILE), slice(hd * V_DIM, (hd + 1) * V_DIM))
             for hd in range(ATTN_HEADS_PER_STEP) for i in range(q_tiles)]

    @pl.when(pl.program_id(0) == 0)
    def _():
        rows, head = tiles[0]
        _store_scores(q_ref[rows, head], k_ref[:, head], *scores[0])

    lqk = lq_ref[...] * lk_ref[...]
    lam = (jnp.exp(jnp.sum(lqk[0:1, :], axis=-1, keepdims=True))
           - jnp.exp(jnp.sum(lqk[1:2, :], axis=-1, keepdims=True)) + lambda_init)
    out_gain = sg_ref[...] * (1.0 - lambda_init)

    for i, (rows, head) in enumerate(tiles):
        if i + 1 < len(tiles):
            next_rows, next_head = tiles[i + 1]
            _store_scores(q_ref[next_rows, next_head], k_ref[:, next_head], *scores[(i + 1) % 2])
        else:
            _store_scores(qn_ref[...], kn_ref[...], *scores[(i + 1) % 2])
        v_ones = jnp.concatenate([v_ref[:, head], jnp.ones((v_ref.shape[0], V_DIM), _BF16)], axis=1)
        o_ref[rows, head] = _diff_softmax_pv(*scores[i % 2], v_ones, lam, out_gain).astype(o_ref.dtype)


def _attention(q, k, v, lq, lk, sg, *, lambda_init, batch, seq, round_weights=()):
    n = q.shape[0]
    q_tiles = seq // Q_TILE
    assert (q_tiles * ATTN_HEADS_PER_STEP) % 2 == 0
    groups = N_ATTN_HEADS // ATTN_HEADS_PER_STEP
    steps = batch * groups

    def group_block(t):
        return t // groups, t % groups

    def next_step(t):
        return jnp.minimum(t + 1, steps - 1)

    def next_first_tile(t):
        b, g = group_block(next_step(t))
        return b * q_tiles, g * ATTN_HEADS_PER_STEP

    def next_first_head(t):
        b, g = group_block(next_step(t))
        return b, g * ATTN_HEADS_PER_STEP

    whole = pl.BlockSpec((seq, ATTN_HEADS_PER_STEP * V_DIM), group_block)
    small = lambda shape: pl.BlockSpec(shape, lambda t: (0, 0))
    scores = pltpu.VMEM((2, Q_TILE, seq), _F32)
    row_max = pltpu.VMEM((2, Q_TILE, 1), _F32)

    round_in, round_out, round_shapes = _round_specs(round_weights, steps)

    return pl.pallas_call(
        functools.partial(_attn_kernel, lambda_init=lambda_init, n_round=len(round_weights)),
        grid=(steps,),
        in_specs=[
            whole,
            pl.BlockSpec((Q_TILE, V_DIM), next_first_tile),
            whole,
            pl.BlockSpec((seq, V_DIM), next_first_head),
            whole,
            small((2, HEAD_DIM)), small((2, HEAD_DIM)), small((1, V_DIM)),
        ] + round_in,
        out_specs=[whole] + round_out,
        out_shape=[jax.ShapeDtypeStruct((n, D_ATTN), _BF16)] + round_shapes,
        scratch_shapes=[scores, scores, row_max, row_max],
        compiler_params=_params("arbitrary"),
    )(q, q, k, k, v, lq, lk, sg.reshape(1, V_DIM), *[w for w, _ in round_weights])


def _scan_direction(a_scr, b_scr, p_scr, h_scr, *, reverse):
    chains = [(lg, sg) for lg in range(LANE_GROUPS) for sg in range(SCAN_NSEG // SUBLANES)]

    def seg_rows(sg, j):
        return pl.ds(sg * SUBLANES * SCAN_PITCH + j, SUBLANES, stride=SCAN_PITCH)

    def step(jj, carry):
        j = (SCAN_SEG - 1 - jj) if reverse else jj
        hs, ps = carry
        new_h, new_p = [], []
        for c, (lg, sg) in enumerate(chains):
            a = a_scr[lg, seg_rows(sg, j), :]
            b = b_scr[lg, seg_rows(sg, j), :]
            h = a * hs[c] + b
            p = a * ps[c]
            h_scr[lg, seg_rows(sg, j), :] = h
            p_scr[lg, seg_rows(sg, j), :] = p
            new_h.append(h)
            new_p.append(p)
        return tuple(new_h), tuple(new_p)

    zeros = tuple(jnp.zeros((SUBLANES, LANES), _F32) for _ in chains)
    ones = tuple(jnp.ones((SUBLANES, LANES), _F32) for _ in chains)
    h_end, p_tot = lax.fori_loop(0, SCAN_SEG, step, (zeros, ones), unroll=8)

    order = range(SCAN_NSEG - 1, -1, -1) if reverse else range(SCAN_NSEG)
    carries = []
    for lg in range(LANE_GROUPS):
        carry_in = jnp.zeros((1, LANES), _F32)
        per_seg = [None] * SCAN_NSEG
        for s in order:
            c = chains.index((lg, s // SUBLANES))
            r = s % SUBLANES
            per_seg[s] = carry_in
            carry_in = p_tot[c][r:r + 1, :] * carry_in + h_end[c][r:r + 1, :]
        carries.append(per_seg)
    return carries


def _depthwise_conv(x_ref, cw, cb):
    seq = x_ref.shape[0]

    def taps(x, m2, m1, p1):
        return cb + cw[2:3, :] * x + cw[0:1, :] * m2 + cw[1:2, :] * m1 + cw[3:4, :] * p1

    row = lax.broadcasted_iota(jnp.int32, (SUBLANES, x_ref.shape[1]), 0)
    lo, hi = SUBLANES, seq - SUBLANES
    x_top = x_ref[0:lo, :]
    x_bot = x_ref[hi:seq, :]
    return jnp.concatenate([
        taps(x_top,
             jnp.where(row >= 2, pltpu.roll(x_top, 2, axis=0), 0.0),
             jnp.where(row >= 1, pltpu.roll(x_top, 1, axis=0), 0.0),
             x_ref[1:lo + 1, :]),
        taps(x_ref[lo:hi, :], x_ref[lo - 2:hi - 2, :], x_ref[lo - 1:hi - 1, :], x_ref[lo + 1:hi + 1, :]),
        taps(x_bot, x_ref[hi - 2:seq - 2, :], x_ref[hi - 1:seq - 1, :],
             jnp.where(row < SUBLANES - 1, pltpu.roll(x_bot, SUBLANES - 1, axis=0), 0.0)),
    ], axis=0)


def _lru_kernel(xr_ref, gy_ref, cw_ref, cb_ref, wg_ref, bg_ref, lam_ref, o_ref,
                a_scr, b_scr, p_scr, h_scr, sum_scr):
    xc = _depthwise_conv(xr_ref, cw_ref[...], cb_ref[...])
    xcb = xc.astype(_BF16)
    xc_half = 0.5 * xc

    for d in range(2):
        lam = lam_ref[d:d + 1, :]
        half_c = (-0.5 * LRU_C) * (jnp.maximum(-lam, 0.0) + jnp.log1p(jnp.exp(-jnp.abs(lam))))
        for half in range(D_LRU // MXU_DIM):
            cols = slice(half * MXU_DIM, (half + 1) * MXU_DIM)
            xh = xcb[:, cols]
            gr = jnp.dot(xh, wg_ref[d, 0, half], preferred_element_type=_F32)
            gi = jnp.dot(xh, wg_ref[d, 1, half], preferred_element_type=_F32)
            t_r = jnp.tanh(gr + 0.5 * bg_ref[d, 0:1, cols])
            t_i = jnp.tanh(gi + 0.5 * bg_ref[d, 1:2, cols])
            log_a = half_c[:, cols] * t_r + half_c[:, cols]
            a = jnp.exp(log_a)
            one_minus_a2 = jnp.tanh(log_a) * (-1.0 - a * a)
            mult = jnp.where(one_minus_a2 > 0.0, one_minus_a2 * lax.rsqrt(one_minus_a2), 0.0)
            b = mult * ((t_i + 1.0) * xc_half[:, cols])
            for sub in range(MXU_DIM // LANES):
                lg = half * (MXU_DIM // LANES) + sub
                lanes = slice(sub * LANES, (sub + 1) * LANES)
                for s in range(SCAN_NSEG):
                    src = slice(s * SCAN_SEG, (s + 1) * SCAN_SEG)
                    dst = pl.ds(s * SCAN_PITCH, SCAN_SEG)
                    a_scr[lg, dst, :] = a[src, lanes]
                    b_scr[lg, dst, :] = b[src, lanes]
        carries = _scan_direction(a_scr, b_scr, p_scr, h_scr, reverse=(d == 1))
        for lg in range(LANE_GROUPS):
            for s in range(SCAN_NSEG):
                rows = pl.ds(s * SCAN_PITCH, SCAN_SEG)
                dst = (slice(s * SCAN_SEG, (s + 1) * SCAN_SEG), slice(lg * LANES, (lg + 1) * LANES))
                hseg = h_scr[lg, rows, :] + p_scr[lg, rows, :] * carries[lg][s]
                if d == 0:
                    sum_scr[dst] = hseg
                else:
                    o_ref[dst] = ((sum_scr[dst] + hseg) * gy_ref[dst]).astype(o_ref.dtype)


def _recurrent_group(xr, gy, conv_w, conv_b, wg_bd, b_gate, lam, *, batch, seq):
    n = xr.shape[0]
    tok = pl.BlockSpec((seq, D_LRU), lambda b: (b, 0))
    scan_buf = pltpu.VMEM((LANE_GROUPS, SCAN_NSEG * SCAN_PITCH, LANES), _F32)
    return pl.pallas_call(
        _lru_kernel,
        grid=(batch,),
        in_specs=[
            tok, tok,
            pl.BlockSpec((CONV_WIDTH, D_LRU), lambda b: (0, 0)),
            pl.BlockSpec((1, D_LRU), lambda b: (0, 0)),
            pl.BlockSpec(wg_bd.shape, lambda b: (0, 0, 0, 0, 0)),
            pl.BlockSpec((2, 2, D_LRU), lambda b: (0, 0, 0)),
            pl.BlockSpec((2, D_LRU), lambda b: (0, 0)),
        ],
        out_specs=tok,
        out_shape=jax.ShapeDtypeStruct((n, D_LRU), _BF16),
        scratch_shapes=[scan_buf] * 4 + [pltpu.VMEM((seq, D_LRU), _F32)],
        compiler_params=_params("parallel"),
    )(xr, gy, conv_w, conv_b.reshape(1, D_LRU), wg_bd, b_gate, lam)


def _block_diag_gates(w_gate):
    per_tile = MXU_DIM // LRU_BLOCK
    w = w_gate.reshape(2, 2, N_LRU_BLOCKS // per_tile, per_tile, LRU_BLOCK, LRU_BLOCK)
    eye = jnp.eye(per_tile, dtype=w.dtype)
    bd = jnp.einsum('dghnij,nm->dghnimj', w, eye)
    return bd.reshape(2, 2, N_LRU_BLOCKS // per_tile, MXU_DIM, MXU_DIM)


def kernel(x, c, positions, w_ada, b_ada, norm_g, ffn1_w_in, ffn1_w_out, ffn2_w_in, ffn2_w_out,
           w_mix_in, w_mix_out, lambda_q, lambda_k, subln_g, conv_w, conv_b,
           lru_w_gate, lru_b_gate, lru_lambda):
    batch, seq, d = x.shape
    depth = w_ada.shape[0]
    assert (d, seq % TOKEN_TILE, seq % Q_TILE) == (D_MODEL, 0, 0)
    assert seq == SCAN_SEG * SCAN_NSEG

    ada = _ada_all_layers(c, w_ada, b_ada).reshape(depth, batch, 3 * N_SUB, d)
    cos_t, sin_t = _rope_tables(positions)
    x2 = x.reshape(batch * seq, d)
    ffn1_in, ffn1_out = ffn1_w_in[0].astype(_BF16), ffn1_w_out[0].astype(_BF16)

    for l in range(depth):
        lambda_init = 0.8 - 0.6 * math.exp(-0.3 * l)
        x2, ffn2_in, ffn2_out = _ffn_sublayer(x2, ada, norm_g, ffn1_in, ffn1_out, l=l, j=0, res_w=0.5, seq=seq,
                                              round_weights=[(ffn2_w_in, l), (ffn2_w_out, l)])
        q, k, v, xr, gy = _mix_in(x2, ada, norm_g, w_mix_in, cos_t, sin_t, l=l, seq=seq)
        later = [(ffn1_w_in, l + 1), (ffn1_w_out, l + 1)] if l + 1 < depth else []
        attn, *next_ffn1 = _attention(q, k, v, lambda_q[l], lambda_k[l], subln_g[l],
                                      lambda_init=lambda_init, batch=batch, seq=seq, round_weights=later)
        if next_ffn1:
            ffn1_in, ffn1_out = next_ffn1
        rec = _recurrent_group(xr, gy, conv_w[l], conv_b[l],
                               (0.5 * _block_diag_gates(lru_w_gate[l])).astype(_BF16),
                               lru_b_gate[l], lru_lambda[l], batch=batch, seq=seq)
        x2 = _ffn_sublayer(x2, ada, norm_g, ffn2_in, ffn2_out, l=l, j=2, res_w=0.5, seq=seq,
                           mixer=(attn, rec, w_mix_out))
    return x2.reshape(batch, seq, d)
```

```python
import functools
import math

import jax
import jax.numpy as jnp
from jax import lax
from jax.experimental import pallas as pl
from jax.experimental.pallas import tpu as pltpu

D_MODEL = 1024
D_ATTN = 512
D_LRU = 512
HEAD_DIM = 64
V_DIM = 128
N_ATTN_HEADS = 4
N_LRU_BLOCKS = 8
LRU_BLOCK = 64
CONV_WIDTH = 4
LRU_C = 8.0
MIX_IN = 3 * D_ATTN + 2 * D_LRU
D_FF = 2816
ROPE_THETA = 10000.0
EPS = 1e-6
N_SUB = 3
LOG2_E = 1.4426950408889634

LANES = 128
SUBLANES = 8
BF16_SUBLANES = 16
MXU_DIM = 256
VMEM_LIMIT_BYTES = 56 * 1024 * 1024

TOKEN_TILE = 1024
FFN_SUB_TILE = 256
Q_TILE = 512
ATTN_HEADS_PER_STEP = 1
FF_CHUNK = MXU_DIM
ADA_COLS = 4608

SCAN_SEG = 64
SCAN_NSEG = 32
SCAN_PITCH = 68
LANE_GROUPS = D_LRU // LANES

_F32 = jnp.float32
_BF16 = jnp.bfloat16


def _params(*sem):
    return pltpu.CompilerParams(dimension_semantics=sem, vmem_limit_bytes=VMEM_LIMIT_BYTES)


def _rms(x):
    return lax.rsqrt(jnp.mean(x * x, axis=-1, keepdims=True) + EPS)


def _ada_rows(ada_ref, j):
    shift = ada_ref[0, 3 * j:3 * j + 1, :]
    scale = ada_ref[0, 3 * j + 1:3 * j + 2, :]
    gate = ada_ref[0, 3 * j + 2:3 * j + 3, :]
    return shift, scale, gate


def _prenorm(x, g_pre, scale, shift):
    return (x * _rms(x)) * (g_pre * (1.0 + scale)) + shift


def _round_specs(round_weights, steps):
    ins, outs, shapes = [], [], []
    for w, l in round_weights:
        rows, cols = w.shape[1:]
        chunks = steps
        while rows % chunks or (rows // chunks) % BF16_SUBLANES:
            chunks //= 2
        rep = steps // chunks
        ins.append(pl.BlockSpec((None, rows // chunks, cols), lambda t, l=l, rep=rep: (l, t // rep, 0)))
        outs.append(pl.BlockSpec((rows // chunks, cols), lambda t, rep=rep: (t // rep, 0)))
        shapes.append(jax.ShapeDtypeStruct((rows, cols), _BF16))
    return ins, outs, shapes


def _round_blocks(f32_refs, bf16_refs):
    for src, dst in zip(f32_refs, bf16_refs):
        dst[...] = src[...].astype(dst.dtype)


def _ada_kernel(c_ref, w_ref, b_ref, o_ref):
    c = c_ref[...]
    cond = (c * jax.nn.sigmoid(c)).astype(_BF16)
    w = w_ref[0].astype(_BF16)
    o_ref[0] = jnp.dot(cond, w, preferred_element_type=_F32) + b_ref[0]


def _ada_all_layers(c, w_ada, b_ada):
    depth, d, n = w_ada.shape
    b = c.shape[0]
    return pl.pallas_call(
        _ada_kernel,
        grid=(depth, n // ADA_COLS),
        in_specs=[
            pl.BlockSpec((b, d), lambda l, i: (0, 0)),
            pl.BlockSpec((1, d, ADA_COLS), lambda l, i: (l, 0, i)),
            pl.BlockSpec((1, 1, ADA_COLS), lambda l, i: (l, 0, i)),
        ],
        out_specs=pl.BlockSpec((1, b, ADA_COLS), lambda l, i: (l, 0, i)),
        out_shape=jax.ShapeDtypeStruct((depth, b, n), _F32),
        compiler_params=_params("parallel", "parallel"),
    )(c, w_ada, b_ada.reshape(depth, 1, n))


def _rope_kernel(*refs, n_round):
    pos_ref, inv_ref, sign_ref = refs[:3]
    cos_ref, sin_ref = refs[3 + n_round:5 + n_round]
    _round_blocks(refs[3:3 + n_round], refs[5 + n_round:])
    ang = pos_ref[...].astype(_F32) * inv_ref[...]
    cos_ref[...] = jnp.cos(ang)
    sin_ref[...] = jnp.sin(ang) * sign_ref[...]


def _rope_tables(positions, round_weights=()):
    n = positions.size
    inv = ROPE_THETA ** (-jnp.arange(0, HEAD_DIM, 2, dtype=_F32) / HEAD_DIM)
    reps = LANES // (HEAD_DIM // 2)
    inv_row = jnp.tile(inv, reps).reshape(1, LANES)
    half = jnp.concatenate([-jnp.ones((HEAD_DIM // 2,), _F32), jnp.ones((HEAD_DIM // 2,), _F32)])
    sign_row = jnp.tile(half, LANES // HEAD_DIM).reshape(1, LANES)
    row = pl.BlockSpec((1, LANES), lambda i: (0, 0))
    tab = pl.BlockSpec((TOKEN_TILE, LANES), lambda i: (i, 0))
    round_in, round_out, round_shapes = _round_specs(round_weights, n // TOKEN_TILE)
    return pl.pallas_call(
        functools.partial(_rope_kernel, n_round=len(round_weights)),
        grid=(n // TOKEN_TILE,),
        in_specs=[pl.BlockSpec((TOKEN_TILE, 1), lambda i: (i, 0)), row, row] + round_in,
        out_specs=[tab, tab] + round_out,
        out_shape=[jax.ShapeDtypeStruct((n, LANES), _F32)] * 2 + round_shapes,
        compiler_params=_params("arbitrary"),
    )(positions.reshape(n, 1), inv_row, sign_row, *[w for w, _ in round_weights])


def _ffn_kernel(*refs, j, res_w, after_mixer, n_round):
    n_in = 8 if after_mixer else 5
    if after_mixer:
        x_ref, attn_ref, rec_ref, wmix_ref, ada_ref, g_ref, win_ref, wout_ref = refs[:n_in]
    else:
        x_ref, ada_ref, g_ref, win_ref, wout_ref = refs[:n_in]
    o_ref, act_ref = refs[n_in + n_round], refs[-1]
    _round_blocks(refs[n_in:n_in + n_round], refs[n_in + n_round + 1:n_in + 2 * n_round + 1])
    shift, scale, gate = _ada_rows(ada_ref, j)
    pieces = [slice(r * FFN_SUB_TILE, (r + 1) * FFN_SUB_TILE) for r in range(TOKEN_TILE // FFN_SUB_TILE)]

    xs, hs = [], []
    for rows in pieces:
        x = x_ref[rows, :]
        if after_mixer:
            _, _, mix_gate = _ada_rows(ada_ref, 1)
            m = (jnp.dot(attn_ref[rows, :], wmix_ref[:D_ATTN, :].astype(_BF16), preferred_element_type=_F32)
                 + jnp.dot(rec_ref[rows, :], wmix_ref[D_ATTN:, :].astype(_BF16), preferred_element_type=_F32))
            x = x + mix_gate * ((m * _rms(m)) * g_ref[3:4, :])
        xs.append(x)
        hs.append(_prenorm(x, g_ref[2 * j:2 * j + 1, :], scale, shift).astype(_BF16))

    def finish(rows, x, y):
        y = (y * _rms(y)) * g_ref[2 * j + 1:2 * j + 2, :]
        o_ref[rows, :] = x + (res_w * gate) * y

    pending = None
    for rows, x, h in zip(pieces, xs, hs):
        for c in range(D_FF // FF_CHUNK):
            lo = c * FF_CHUNK
            g = jnp.dot(h, win_ref[:, lo:lo + FF_CHUNK], preferred_element_type=_F32)
            u = jnp.dot(h, win_ref[:, D_FF + lo:D_FF + lo + FF_CHUNK], preferred_element_type=_F32)
            act_ref[rows, lo:lo + FF_CHUNK] = (g * jax.nn.sigmoid(g) * u).astype(_BF16)
        if pending is not None:
            finish(*pending)
        pending = (rows, x, jnp.dot(act_ref[rows, :], wout_ref[...], preferred_element_type=_F32))
    finish(*pending)


def _layer_block(shape, l, **kw):
    zeros = (0,) * len(shape)
    return pl.BlockSpec((None,) + tuple(shape), lambda i: (l,) + zeros, **kw)


def _ada_block(l, tiles_per_seq):
    return pl.BlockSpec((None, 1, 3 * N_SUB, D_MODEL), lambda i: (l, i // tiles_per_seq, 0, 0))


def _ffn_sublayer(x2, ada, norm_g, w_in, w_out, *, l, j, res_w, seq, mixer=None, round_weights=()):
    n, d = x2.shape
    tiles_per_seq = seq // TOKEN_TILE
    resident = dict(pipeline_mode=pl.Buffered(1))
    tok = pl.BlockSpec((TOKEN_TILE, d), lambda i: (i, 0))
    in_specs, args = [tok], [x2]
    if mixer is not None:
        col = pl.BlockSpec((TOKEN_TILE, D_ATTN), lambda i: (i, 0))
        in_specs += [col, col, _layer_block((d, d), l, **resident)]
        args += list(mixer)
    in_specs += [
        _ada_block(l, tiles_per_seq),
        _layer_block((2 * N_SUB, d), l),
        pl.BlockSpec((d, 2 * D_FF), lambda i: (0, 0), **resident),
        pl.BlockSpec((D_FF, d), lambda i: (0, 0), **resident),
    ]
    args += [ada, norm_g, w_in, w_out]
    round_in, round_out, round_shapes = _round_specs(round_weights, n // TOKEN_TILE)
    out = pl.pallas_call(
        functools.partial(_ffn_kernel, j=j, res_w=res_w, after_mixer=mixer is not None,
                          n_round=len(round_weights)),
        grid=(n // TOKEN_TILE,),
        in_specs=in_specs + round_in,
        out_specs=[tok] + round_out,
        out_shape=[jax.ShapeDtypeStruct((n, d), _F32)] + round_shapes,
        scratch_shapes=[pltpu.VMEM((TOKEN_TILE, D_FF), _BF16)],
        compiler_params=_params("arbitrary" if round_weights else "parallel"),
    )(*args, *[w for w, _ in round_weights])
    return out if round_weights else out[0]


def _rope(t, cos, sin_signed):
    lane = lax.broadcasted_iota(jnp.int32, t.shape, 1)
    first_half = (lane % HEAD_DIM) < (HEAD_DIM // 2)
    partner = jnp.where(first_half,
                        pltpu.roll(t, LANES - HEAD_DIM // 2, axis=1),
                        pltpu.roll(t, HEAD_DIM // 2, axis=1))
    return t * cos + partner * sin_signed


def _mix_in_kernel(x_ref, ada_ref, g_ref, w_ref, cos_ref, sin_ref,
                   q_ref, k_ref, v_ref, xr_ref, gy_ref):
    j = 1
    x = x_ref[...]
    shift, scale, _ = _ada_rows(ada_ref, j)
    h = _prenorm(x, g_ref[2 * j:2 * j + 1, :], scale, shift).astype(_BF16)
    cos = cos_ref[...]
    sin = sin_ref[...]
    q_scale = HEAD_DIM ** -0.5 * LOG2_E

    def proj(lo, width):
        return jnp.dot(h, w_ref[:, lo:lo + width].astype(_BF16), preferred_element_type=_F32)

    q = proj(0, D_ATTN)
    k = proj(D_ATTN, D_ATTN)
    for hd in range(N_ATTN_HEADS):
        head = slice(hd * LANES, (hd + 1) * LANES)
        q_ref[:, head] = (_rope(q[:, head], cos, sin) * q_scale).astype(_BF16)
        k_ref[:, head] = _rope(k[:, head], cos, sin).astype(_BF16)
    v_ref[...] = proj(2 * D_ATTN, D_ATTN).astype(_BF16)
    xr_ref[...] = proj(3 * D_ATTN, D_LRU)
    gy_ref[...] = jax.nn.gelu(proj(3 * D_ATTN + D_LRU, D_LRU))


def _mix_in(x2, ada, norm_g, w_in, cos_t, sin_t, *, l, seq):
    n, d = x2.shape
    tiles_per_seq = seq // TOKEN_TILE
    half = lambda dt: jax.ShapeDtypeStruct((n, D_ATTN), dt)
    col = pl.BlockSpec((TOKEN_TILE, D_ATTN), lambda i: (i, 0))
    tab = pl.BlockSpec((TOKEN_TILE, LANES), lambda i: (i, 0))
    return pl.pallas_call(
        _mix_in_kernel,
        grid=(n // TOKEN_TILE,),
        in_specs=[
            pl.BlockSpec((TOKEN_TILE, d), lambda i: (i, 0)),
            _ada_block(l, tiles_per_seq),
            _layer_block((2 * N_SUB, d), l),
            _layer_block((d, MIX_IN), l, pipeline_mode=pl.Buffered(1)),
            tab, tab,
        ],
        out_specs=[col] * 5,
        out_shape=[half(_BF16), half(_BF16), half(_BF16), half(_F32), half(_F32)],
        compiler_params=_params("parallel"),
    )(x2, ada, norm_g, w_in, cos_t, sin_t)


def _store_scores(q, k, s_ref, m_ref):
    lane = lax.broadcasted_iota(jnp.int32, q.shape, 1)
    first_map = lane < HEAD_DIM
    zero = jnp.zeros_like(q)
    nt = (((1,), (1,)), ((), ()))
    for m, qm in enumerate((jnp.where(first_map, q, zero), jnp.where(first_map, zero, q))):
        s = lax.dot_general(qm, k, nt, preferred_element_type=_F32)
        s_ref[m] = s
        m_ref[m] = jnp.max(s, axis=-1, keepdims=True)


def _diff_softmax_pv(s_ref, m_ref, v_ones, lam, out_gain):
    maps = []
    for m in range(2):
        p = jnp.exp2(s_ref[m] - m_ref[m])
        pv = jnp.dot(p.astype(_BF16), v_ones, preferred_element_type=_F32)
        maps.append(pv[:, :V_DIM] * (1.0 / pv[:, V_DIM:]))
    o = maps[0] - lam * maps[1]
    return (o * _rms(o)) * out_gain


def _attn_kernel(*refs, lambda_init, n_round):
    q_ref, qn_ref, k_ref, kn_ref, v_ref, lq_ref, lk_ref, sg_ref = refs[:8]
    f32_weights = refs[8:8 + n_round]
    o_ref = refs[8 + n_round]
    bf16_weights = refs[9 + n_round:9 + 2 * n_round]
    s0_ref, s1_ref, m0_ref, m1_ref = refs[9 + 2 * n_round:]

    _round_blocks(f32_weights, bf16_weights)

    q_tiles = q_ref.shape[0] // Q_TILE
    scores = ((s0_ref, m0_ref), (s1_ref, m1_ref))
    tiles = [(slice(i * Q_TILE, (i + 1) * Q_TILE), slice(hd * V_DIM, (hd + 1) * V_DIM))
             for hd in range(ATTN_HEADS_PER_STEP) for i in range(q_tiles)]

    @pl.when(pl.program_id(0) == 0)
    def _():
        rows, head = tiles[0]
        _store_scores(q_ref[rows, head], k_ref[:, head], *scores[0])

    lqk = lq_ref[...] * lk_ref[...]
    lam = (jnp.exp(jnp.sum(lqk[0:1, :], axis=-1, keepdims=True))
           - jnp.exp(jnp.sum(lqk[1:2, :], axis=-1, keepdims=True)) + lambda_init)
    out_gain = sg_ref[...] * (1.0 - lambda_init)

    for i, (rows, head) in enumerate(tiles):
        if i + 1 < len(tiles):
            next_rows, next_head = tiles[i + 1]
            _store_scores(q_ref[next_rows, next_head], k_ref[:, next_head], *scores[(i + 1) % 2])
        else:
            _store_scores(qn_ref[...], kn_ref[...], *scores[(i + 1) % 2])
        v_ones = jnp.concatenate([v_ref[:, head], jnp.ones((v_ref.shape[0], V_DIM), _BF16)], axis=1)
        o_ref[rows, head] = _diff_softmax_pv(*scores[i % 2], v_ones, lam, out_gain).astype(o_ref.dtype)


def _attention(q, k, v, lq, lk, sg, *, lambda_init, batch, seq, round_weights=()):
    n = q.shape[0]
    q_tiles = seq // Q_TILE
    assert (q_tiles * ATTN_HEADS_PER_STEP) % 2 == 0
    groups = N_ATTN_HEADS // ATTN_HEADS_PER_STEP
    steps = batch * groups

    def group_block(t):
        return t // groups, t % groups

    def next_step(t):
        return jnp.minimum(t + 1, steps - 1)

    def next_first_tile(t):
        b, g = group_block(next_step(t))
        return b * q_tiles, g * ATTN_HEADS_PER_STEP

    def next_first_head(t):
        b, g = group_block(next_step(t))
        return b, g * ATTN_HEADS_PER_STEP

    whole = pl.BlockSpec((seq, ATTN_HEADS_PER_STEP * V_DIM), group_block)
    small = lambda shape: pl.BlockSpec(shape, lambda t: (0, 0))
    scores = pltpu.VMEM((2, Q_TILE, seq), _F32)
    row_max = pltpu.VMEM((2, Q_TILE, 1), _F32)

    round_in, round_out, round_shapes = _round_specs(round_weights, steps)

    return pl.pallas_call(
        functools.partial(_attn_kernel, lambda_init=lambda_init, n_round=len(round_weights)),
        grid=(steps,),
        in_specs=[
            whole,
            pl.BlockSpec((Q_TILE, V_DIM), next_first_tile),
            whole,
            pl.BlockSpec((seq, V_DIM), next_first_head),
            whole,
            small((2, HEAD_DIM)), small((2, HEAD_DIM)), small((1, V_DIM)),
        ] + round_in,
        out_specs=[whole] + round_out,
        out_shape=[jax.ShapeDtypeStruct((n, D_ATTN), _BF16)] + round_shapes,
        scratch_shapes=[scores, scores, row_max, row_max],
        compiler_params=_params("arbitrary"),
    )(q, q, k, k, v, lq, lk, sg.reshape(1, V_DIM), *[w for w, _ in round_weights])


def _scan_direction(a_scr, b_scr, p_scr, h_scr, *, reverse):
    chains = [(lg, sg) for lg in range(LANE_GROUPS) for sg in range(SCAN_NSEG // SUBLANES)]

    def seg_rows(sg, j):
        return pl.ds(sg * SUBLANES * SCAN_PITCH + j, SUBLANES, stride=SCAN_PITCH)

    def step(jj, carry):
        j = (SCAN_SEG - 1 - jj) if reverse else jj
        hs, ps = carry
        new_h, new_p = [], []
        for c, (lg, sg) in enumerate(chains):
            a = a_scr[lg, seg_rows(sg, j), :]
            b = b_scr[lg, seg_rows(sg, j), :]
            h = a * hs[c] + b
            p = a * ps[c]
            h_scr[lg, seg_rows(sg, j), :] = h
            p_scr[lg, seg_rows(sg, j), :] = p
            new_h.append(h)
            new_p.append(p)
        return tuple(new_h), tuple(new_p)

    zeros = tuple(jnp.zeros((SUBLANES, LANES), _F32) for _ in chains)
    ones = tuple(jnp.ones((SUBLANES, LANES), _F32) for _ in chains)
    h_end, p_tot = lax.fori_loop(0, SCAN_SEG, step, (zeros, ones), unroll=8)

    order = range(SCAN_NSEG - 1, -1, -1) if reverse else range(SCAN_NSEG)
    carries = []
    for lg in range(LANE_GROUPS):
        carry_in = jnp.zeros((1, LANES), _F32)
        per_seg = [None] * SCAN_NSEG
        for s in order:
            c = chains.index((lg, s // SUBLANES))
            r = s % SUBLANES
            per_seg[s] = carry_in
            carry_in = p_tot[c][r:r + 1, :] * carry_in + h_end[c][r:r + 1, :]
        carries.append(per_seg)
    return carries


def _depthwise_conv(x_ref, cw, cb):
    seq = x_ref.shape[0]

    def taps(x, m2, m1, p1):
        return cb + cw[2:3, :] * x + cw[0:1, :] * m2 + cw[1:2, :] * m1 + cw[3:4, :] * p1

    row = lax.broadcasted_iota(jnp.int32, (SUBLANES, x_ref.shape[1]), 0)
    lo, hi = SUBLANES, seq - SUBLANES
    x_top = x_ref[0:lo, :]
    x_bot = x_ref[hi:seq, :]
    return jnp.concatenate([
        taps(x_top,
             jnp.where(row >= 2, pltpu.roll(x_top, 2, axis=0), 0.0),
             jnp.where(row >= 1, pltpu.roll(x_top, 1, axis=0), 0.0),
             x_ref[1:lo + 1, :]),
        taps(x_ref[lo:hi, :], x_ref[lo - 2:hi - 2, :], x_ref[lo - 1:hi - 1, :], x_ref[lo + 1:hi + 1, :]),
        taps(x_bot, x_ref[hi - 2:seq - 2, :], x_ref[hi - 1:seq - 1, :],
             jnp.where(row < SUBLANES - 1, pltpu.roll(x_bot, SUBLANES - 1, axis=0), 0.0)),
    ], axis=0)


def _lru_kernel(xr_ref, gy_ref, cw_ref, cb_ref, wg_ref, bg_ref, lam_ref, o_ref,
                a_scr, b_scr, p_scr, h_scr, sum_scr):
    xc = _depthwise_conv(xr_ref, cw_ref[...], cb_ref[...])
    xcb = xc.astype(_BF16)
    xc_half = 0.5 * xc

    for d in range(2):
        lam = lam_ref[d:d + 1, :]
        half_c = (-0.5 * LRU_C) * (jnp.maximum(-lam, 0.0) + jnp.log1p(jnp.exp(-jnp.abs(lam))))
        for half in range(D_LRU // MXU_DIM):
            cols = slice(half * MXU_DIM, (half + 1) * MXU_DIM)
            xh = xcb[:, cols]
            gr = jnp.dot(xh, wg_ref[d, 0, half], preferred_element_type=_F32)
            gi = jnp.dot(xh, wg_ref[d, 1, half], preferred_element_type=_F32)
            t_r = jnp.tanh(gr + 0.5 * bg_ref[d, 0:1, cols])
            t_i = jnp.tanh(gi + 0.5 * bg_ref[d, 1:2, cols])
            log_a = half_c[:, cols] * t_r + half_c[:, cols]
            a = jnp.exp(log_a)
            one_minus_a2 = jnp.tanh(log_a) * (-1.0 - a * a)
            mult = jnp.where(one_minus_a2 > 0.0, one_minus_a2 * lax.rsqrt(one_minus_a2), 0.0)
            b = mult * ((t_i + 1.0) * xc_half[:, cols])
            for sub in range(MXU_DIM // LANES):
                lg = half * (MXU_DIM // LANES) + sub
                lanes = slice(sub * LANES, (sub + 1) * LANES)
                for s in range(SCAN_NSEG):
                    src = slice(s * SCAN_SEG, (s + 1) * SCAN_SEG)
                    dst = pl.ds(s * SCAN_PITCH, SCAN_SEG)
                    a_scr[lg, dst, :] = a[src, lanes]
                    b_scr[lg, dst, :] = b[src, lanes]
        carries = _scan_direction(a_scr, b_scr, p_scr, h_scr, reverse=(d == 1))
        for lg in range(LANE_GROUPS):
            for s in range(SCAN_NSEG):
                rows = pl.ds(s * SCAN_PITCH, SCAN_SEG)
                dst = (slice(s * SCAN_SEG, (s + 1) * SCAN_SEG), slice(lg * LANES, (lg + 1) * LANES))
                hseg = h_scr[lg, rows, :] + p_scr[lg, rows, :] * carries[lg][s]
                if d == 0:
                    sum_scr[dst] = hseg
                else:
                    o_ref[dst] = ((sum_scr[dst] + hseg) * gy_ref[dst]).astype(o_ref.dtype)


def _recurrent_group(xr, gy, conv_w, conv_b, wg_bd, b_gate, lam, *, batch, seq):
    n = xr.shape[0]
    tok = pl.BlockSpec((seq, D_LRU), lambda b: (b, 0))
    scan_buf = pltpu.VMEM((LANE_GROUPS, SCAN_NSEG * SCAN_PITCH, LANES), _F32)
    return pl.pallas_call(
        _lru_kernel,
        grid=(batch,),
        in_specs=[
            tok, tok,
            pl.BlockSpec((CONV_WIDTH, D_LRU), lambda b: (0, 0)),
            pl.BlockSpec((1, D_LRU), lambda b: (0, 0)),
            pl.BlockSpec(wg_bd.shape, lambda b: (0, 0, 0, 0, 0)),
            pl.BlockSpec((2, 2, D_LRU), lambda b: (0, 0, 0)),
            pl.BlockSpec((2, D_LRU), lambda b: (0, 0)),
        ],
        out_specs=tok,
        out_shape=jax.ShapeDtypeStruct((n, D_LRU), _BF16),
        scratch_shapes=[scan_buf] * 4 + [pltpu.VMEM((seq, D_LRU), _F32)],
        compiler_params=_params("parallel"),
    )(xr, gy, conv_w, conv_b.reshape(1, D_LRU), wg_bd, b_gate, lam)


def _block_diag_gates(w_gate):
    per_tile = MXU_DIM // LRU_BLOCK
    w = w_gate.reshape(2, 2, N_LRU_BLOCKS // per_tile, per_tile, LRU_BLOCK, LRU_BLOCK)
    eye = jnp.eye(per_tile, dtype=w.dtype)
    bd = jnp.einsum('dghnij,nm->dghnimj', w, eye)
    return bd.reshape(2, 2, N_LRU_BLOCKS // per_tile, MXU_DIM, MXU_DIM)


def kernel(x, c, positions, w_ada, b_ada, norm_g, ffn1_w_in, ffn1_w_out, ffn2_w_in, ffn2_w_out,
           w_mix_in, w_mix_out, lambda_q, lambda_k, subln_g, conv_w, conv_b,
           lru_w_gate, lru_b_gate, lru_lambda):
    batch, seq, d = x.shape
    depth = w_ada.shape[0]
    assert (d, seq % TOKEN_TILE, seq % Q_TILE) == (D_MODEL, 0, 0)
    assert seq == SCAN_SEG * SCAN_NSEG

    ada = _ada_all_layers(c, w_ada, b_ada).reshape(depth, batch, 3 * N_SUB, d)
    cos_t, sin_t, ffn1_in, ffn1_out = _rope_tables(positions, [(ffn1_w_in, 0), (ffn1_w_out, 0)])
    x2 = x.reshape(batch * seq, d)

    for l in range(depth):
        lambda_init = 0.8 - 0.6 * math.exp(-0.3 * l)
        x2, ffn2_in, ffn2_out = _ffn_sublayer(x2, ada, norm_g, ffn1_in, ffn1_out, l=l, j=0, res_w=0.5, seq=seq,
                                              round_weights=[(ffn2_w_in, l), (ffn2_w_out, l)])
        q, k, v, xr, gy = _mix_in(x2, ada, norm_g, w_mix_in, cos_t, sin_t, l=l, seq=seq)
        later = [(ffn1_w_in, l + 1), (ffn1_w_out, l + 1)] if l + 1 < depth else []
        attn, *next_ffn1 = _attention(q, k, v, lambda_q[l], lambda_k[l], subln_g[l],
                                      lambda_init=lambda_init, batch=batch, seq=seq, round_weights=later)
        if next_ffn1:
            ffn1_in, ffn1_out = next_ffn1
        rec = _recurrent_group(xr, gy, conv_w[l], conv_b[l],
                               (0.5 * _block_diag_gates(lru_w_gate[l])).astype(_BF16),
                               lru_b_gate[l], lru_lambda[l], batch=batch, seq=seq)
        x2 = _ffn_sublayer(x2, ada, norm_g, ffn2_in, ffn2_out, l=l, j=2, res_w=0.5, seq=seq,
                           mixer=(attn, rec, w_mix_out))
    return x2.reshape(batch, seq, d)
```

```python
import functools
import math

import jax
import jax.numpy as jnp
from jax import lax
from jax.experimental import pallas as pl
from jax.experimental.pallas import tpu as pltpu

D_MODEL = 1024
D_ATTN = 512
D_LRU = 512
HEAD_DIM = 64
V_DIM = 128
N_ATTN_HEADS = 4
N_LRU_BLOCKS = 8
LRU_BLOCK = 64
CONV_WIDTH = 4
LRU_C = 8.0
MIX_IN = 3 * D_ATTN + 2 * D_LRU
D_FF = 2816
ROPE_THETA = 10000.0
EPS = 1e-6
N_SUB = 3
LOG2_E = 1.4426950408889634

LANES = 128
SUBLANES = 8
BF16_SUBLANES = 16
MXU_DIM = 256
VMEM_LIMIT_BYTES = 56 * 1024 * 1024

TOKEN_TILE = 1024
FFN_SUB_TILE = 256
Q_TILE = 512
ATTN_HEADS_PER_STEP = 1
FF_CHUNK = MXU_DIM
ADA_COLS = 4608

SCAN_SEG = 64
SCAN_NSEG = 32
SCAN_PITCH = 68
LANE_GROUPS = D_LRU // LANES

_F32 = jnp.float32
_BF16 = jnp.bfloat16


def _params(*sem):
    return pltpu.CompilerParams(dimension_semantics=sem, vmem_limit_bytes=VMEM_LIMIT_BYTES)


def _rms(x):
    return lax.rsqrt(jnp.mean(x * x, axis=-1, keepdims=True) + EPS)


def _ada_rows(ada_ref, j):
    shift = ada_ref[0, 3 * j:3 * j + 1, :]
    scale = ada_ref[0, 3 * j + 1:3 * j + 2, :]
    gate = ada_ref[0, 3 * j + 2:3 * j + 3, :]
    return shift, scale, gate


def _prenorm(x, g_pre, scale, shift):
    return (x * _rms(x)) * (g_pre * (1.0 + scale)) + shift


def _round_specs(round_weights, steps):
    ins, outs, shapes = [], [], []
    for w, l in round_weights:
        rows, cols = w.shape[1:]
        chunks = steps
        while rows % chunks or (rows // chunks) % BF16_SUBLANES:
            chunks //= 2
        rep = steps // chunks
        ins.append(pl.BlockSpec((None, rows // chunks, cols), lambda t, l=l, rep=rep: (l, t // rep, 0)))
        outs.append(pl.BlockSpec((rows // chunks, cols), lambda t, rep=rep: (t // rep, 0)))
        shapes.append(jax.ShapeDtypeStruct((rows, cols), _BF16))
    return ins, outs, shapes


def _round_blocks(f32_refs, bf16_refs):
    for src, dst in zip(f32_refs, bf16_refs):
        dst[...] = src[...].astype(dst.dtype)


def _ada_kernel(c_ref, w_ref, b_ref, o_ref):
    c = c_ref[...]
    cond = (c * jax.nn.sigmoid(c)).astype(_BF16)
    w = w_ref[0].astype(_BF16)
    o_ref[0] = jnp.dot(cond, w, preferred_element_type=_F32) + b_ref[0]


def _ada_all_layers(c, w_ada, b_ada):
    depth, d, n = w_ada.shape
    b = c.shape[0]
    return pl.pallas_call(
        _ada_kernel,
        grid=(depth, n // ADA_COLS),
        in_specs=[
            pl.BlockSpec((b, d), lambda l, i: (0, 0)),
            pl.BlockSpec((1, d, ADA_COLS), lambda l, i: (l, 0, i)),
            pl.BlockSpec((1, 1, ADA_COLS), lambda l, i: (l, 0, i)),
        ],
        out_specs=pl.BlockSpec((1, b, ADA_COLS), lambda l, i: (l, 0, i)),
        out_shape=jax.ShapeDtypeStruct((depth, b, n), _F32),
        compiler_params=_params("parallel", "parallel"),
    )(c, w_ada, b_ada.reshape(depth, 1, n))


def _rope_kernel(*refs, n_round):
    pos_ref, inv_ref, sign_ref = refs[:3]
    cos_ref, sin_ref = refs[3 + n_round:5 + n_round]
    _round_blocks(refs[3:3 + n_round], refs[5 + n_round:])
    ang = pos_ref[...].astype(_F32) * inv_ref[...]
    cos_ref[...] = jnp.cos(ang)
    sin_ref[...] = jnp.sin(ang) * sign_ref[...]


def _rope_tables(positions, round_weights=()):
    n = positions.size
    inv = ROPE_THETA ** (-jnp.arange(0, HEAD_DIM, 2, dtype=_F32) / HEAD_DIM)
    reps = LANES // (HEAD_DIM // 2)
    inv_row = jnp.tile(inv, reps).reshape(1, LANES)
    half = jnp.concatenate([-jnp.ones((HEAD_DIM // 2,), _F32), jnp.ones((HEAD_DIM // 2,), _F32)])
    sign_row = jnp.tile(half, LANES // HEAD_DIM).reshape(1, LANES)
    row = pl.BlockSpec((1, LANES), lambda i: (0, 0))
    tab = pl.BlockSpec((TOKEN_TILE, LANES), lambda i: (i, 0))
    round_in, round_out, round_shapes = _round_specs(round_weights, n // TOKEN_TILE)
    return pl.pallas_call(
        functools.partial(_rope_kernel, n_round=len(round_weights)),
        grid=(n // TOKEN_TILE,),
        in_specs=[pl.BlockSpec((TOKEN_TILE, 1), lambda i: (i, 0)), row, row] + round_in,
        out_specs=[tab, tab] + round_out,
        out_shape=[jax.ShapeDtypeStruct((n, LANES), _F32)] * 2 + round_shapes,
        compiler_params=_params("arbitrary"),
    )(positions.reshape(n, 1), inv_row, sign_row, *[w for w, _ in round_weights])


def _ffn_kernel(*refs, j, res_w, after_mixer, n_round):
    n_in = 8 if after_mixer else 5
    if after_mixer:
        x_ref, attn_ref, rec_ref, wmix_ref, ada_ref, g_ref, win_ref, wout_ref = refs[:n_in]
    else:
        x_ref, ada_ref, g_ref, win_ref, wout_ref = refs[:n_in]
    o_ref, act_ref = refs[n_in + n_round], refs[-1]
    _round_blocks(refs[n_in:n_in + n_round], refs[n_in + n_round + 1:n_in + 2 * n_round + 1])
    shift, scale, gate = _ada_rows(ada_ref, j)
    pieces = [slice(r * FFN_SUB_TILE, (r + 1) * FFN_SUB_TILE) for r in range(TOKEN_TILE // FFN_SUB_TILE)]

    xs, hs = [], []
    for rows in pieces:
        x = x_ref[rows, :]
        if after_mixer:
            _, _, mix_gate = _ada_rows(ada_ref, 1)
            m = (jnp.dot(attn_ref[rows, :], wmix_ref[:D_ATTN, :].astype(_BF16), preferred_element_type=_F32)
                 + jnp.dot(rec_ref[rows, :], wmix_ref[D_ATTN:, :].astype(_BF16), preferred_element_type=_F32))
            x = x + mix_gate * ((m * _rms(m)) * g_ref[3:4, :])
        xs.append(x)
        hs.append(_prenorm(x, g_ref[2 * j:2 * j + 1, :], scale, shift).astype(_BF16))

    def finish(rows, x, y):
        y = (y * _rms(y)) * g_ref[2 * j + 1:2 * j + 2, :]
        o_ref[rows, :] = x + (res_w * gate) * y

    pending = None
    for rows, x, h in zip(pieces, xs, hs):
        for c in range(D_FF // FF_CHUNK):
            lo = c * FF_CHUNK
            g = jnp.dot(h, win_ref[:, lo:lo + FF_CHUNK], preferred_element_type=_F32)
            u = jnp.dot(h, win_ref[:, D_FF + lo:D_FF + lo + FF_CHUNK], preferred_element_type=_F32)
            act_ref[rows, lo:lo + FF_CHUNK] = (g * jax.nn.sigmoid(g) * u).astype(_BF16)
        if pending is not None:
            finish(*pending)
        pending = (rows, x, jnp.dot(act_ref[rows, :], wout_ref[...], preferred_element_type=_F32))
    finish(*pending)


def _layer_block(shape, l, **kw):
    zeros = (0,) * len(shape)
    return pl.BlockSpec((None,) + tuple(shape), lambda i: (l,) + zeros, **kw)


def _ada_block(l, tiles_per_seq):
    return pl.BlockSpec((None, 1, 3 * N_SUB, D_MODEL), lambda i: (l, i // tiles_per_seq, 0, 0))


def _ffn_sublayer(x2, ada, norm_g, w_in, w_out, *, l, j, res_w, seq, mixer=None, round_weights=()):
    n, d = x2.shape
    tiles_per_seq = seq // TOKEN_TILE
    resident = dict(pipeline_mode=pl.Buffered(1))
    tok = pl.BlockSpec((TOKEN_TILE, d), lambda i: (i, 0))
    in_specs, args = [tok], [x2]
    if mixer is not None:
        col = pl.BlockSpec((TOKEN_TILE, D_ATTN), lambda i: (i, 0))
        in_specs += [col, col, _layer_block((d, d), l, **resident)]
        args += list(mixer)
    in_specs += [
        _ada_block(l, tiles_per_seq),
        _layer_block((2 * N_SUB, d), l),
        pl.BlockSpec((d, 2 * D_FF), lambda i: (0, 0), **resident),
        pl.BlockSpec((D_FF, d), lambda i: (0, 0), **resident),
    ]
    args += [ada, norm_g, w_in, w_out]
    round_in, round_out, round_shapes = _round_specs(round_weights, n // TOKEN_TILE)
    out = pl.pallas_call(
        functools.partial(_ffn_kernel, j=j, res_w=res_w, after_mixer=mixer is not None,
                          n_round=len(round_weights)),
        grid=(n // TOKEN_TILE,),
        in_specs=in_specs + round_in,
        out_specs=[tok] + round_out,
        out_shape=[jax.ShapeDtypeStruct((n, d), _F32)] + round_shapes,
        scratch_shapes=[pltpu.VMEM((TOKEN_TILE, D_FF), _BF16)],
        compiler_params=_params("arbitrary" if round_weights else "parallel"),
    )(*args, *[w for w, _ in round_weights])
    return out if round_weights else out[0]


def _rope(t, cos, sin_signed):
    lane = lax.broadcasted_iota(jnp.int32, t.shape, 1)
    first_half = (lane % HEAD_DIM) < (HEAD_DIM // 2)
    partner = jnp.where(first_half,
                        pltpu.roll(t, LANES - HEAD_DIM // 2, axis=1),
                        pltpu.roll(t, HEAD_DIM // 2, axis=1))
    return t * cos + partner * sin_signed


def _mix_in_kernel(x_ref, ada_ref, g_ref, w_ref, cos_ref, sin_ref,
                   q_ref, k_ref, v_ref, xr_ref, gy_ref):
    j = 1
    x = x_ref[...]
    shift, scale, _ = _ada_rows(ada_ref, j)
    h = _prenorm(x, g_ref[2 * j:2 * j + 1, :], scale, shift).astype(_BF16)
    cos = cos_ref[...]
    sin = sin_ref[...]
    q_scale = HEAD_DIM ** -0.5 * LOG2_E

    def proj(lo, width):
        return jnp.dot(h, w_ref[:, lo:lo + width].astype(_BF16), preferred_element_type=_F32)

    q = proj(0, D_ATTN)
    k = proj(D_ATTN, D_ATTN)
    for hd in range(N_ATTN_HEADS):
        head = slice(hd * LANES, (hd + 1) * LANES)
        q_ref[:, head] = (_rope(q[:, head], cos, sin) * q_scale).astype(_BF16)
        k_ref[:, head] = _rope(k[:, head], cos, sin).astype(_BF16)
    v_ref[...] = proj(2 * D_ATTN, D_ATTN).astype(_BF16)
    xr_ref[...] = proj(3 * D_ATTN, D_LRU)
    gy_ref[...] = jax.nn.gelu(proj(3 * D_ATTN + D_LRU, D_LRU))


def _mix_in(x2, ada, norm_g, w_in, cos_t, sin_t, *, l, seq):
    n, d = x2.shape
    tiles_per_seq = seq // TOKEN_TILE
    half = lambda dt: jax.ShapeDtypeStruct((n, D_ATTN), dt)
    col = pl.BlockSpec((TOKEN_TILE, D_ATTN), lambda i: (i, 0))
    tab = pl.BlockSpec((TOKEN_TILE, LANES), lambda i: (i, 0))
    return pl.pallas_call(
        _mix_in_kernel,
        grid=(n // TOKEN_TILE,),
        in_specs=[
            pl.BlockSpec((TOKEN_TILE, d), lambda i: (i, 0)),
            _ada_block(l, tiles_per_seq),
            _layer_block((2 * N_SUB, d), l),
            _layer_block((d, MIX_IN), l, pipeline_mode=pl.Buffered(1)),
            tab, tab,
        ],
        out_specs=[col] * 5,
        out_shape=[half(_BF16), half(_BF16), half(_BF16), half(_F32), half(_F32)],
        compiler_params=_params("parallel"),
    )(x2, ada, norm_g, w_in, cos_t, sin_t)


def _store_scores(q, k, s_ref, m_ref, m):
    lane = lax.broadcasted_iota(jnp.int32, q.shape, 1)
    in_map = (lane < HEAD_DIM) if m == 0 else (lane >= HEAD_DIM)
    nt = (((1,), (1,)), ((), ()))
    s = lax.dot_general(jnp.where(in_map, q, jnp.zeros_like(q)), k, nt, preferred_element_type=_F32)
    s_ref[m] = s
    m_ref[m] = jnp.max(s, axis=-1, keepdims=True)


def _softmax_pv(s_ref, m_ref, v_ones, m):
    p = jnp.exp2(s_ref[m] - m_ref[m])
    pv = jnp.dot(p.astype(_BF16), v_ones, preferred_element_type=_F32)
    return pv[:, :V_DIM] * (1.0 / pv[:, V_DIM:])


def _attn_kernel(*refs, lambda_init, n_round):
    q_ref, qn_ref, k_ref, kn_ref, v_ref, lq_ref, lk_ref, sg_ref = refs[:8]
    f32_weights = refs[8:8 + n_round]
    o_ref = refs[8 + n_round]
    bf16_weights = refs[9 + n_round:9 + 2 * n_round]
    s0_ref, s1_ref, m0_ref, m1_ref = refs[9 + 2 * n_round:]

    _round_blocks(f32_weights, bf16_weights)

    q_tiles = q_ref.shape[0] // Q_TILE
    scores = ((s0_ref, m0_ref), (s1_ref, m1_ref))
    tiles = [(slice(i * Q_TILE, (i + 1) * Q_TILE), slice(hd * V_DIM, (hd + 1) * V_DIM))
             for hd in range(ATTN_HEADS_PER_STEP) for i in range(q_tiles)]

    @pl.when(pl.program_id(0) == 0)
    def _():
        rows, head = tiles[0]
        for m in range(2):
            _store_scores(q_ref[rows, head], k_ref[:, head], *scores[0], m)

    lqk = lq_ref[...] * lk_ref[...]
    lam = (jnp.exp(jnp.sum(lqk[0:1, :], axis=-1, keepdims=True))
           - jnp.exp(jnp.sum(lqk[1:2, :], axis=-1, keepdims=True)) + lambda_init)
    out_gain = sg_ref[...] * (1.0 - lambda_init)

    for i, (rows, head) in enumerate(tiles):
        if i + 1 < len(tiles):
            next_rows, next_head = tiles[i + 1]
            q_next, k_next = q_ref[next_rows, next_head], k_ref[:, next_head]
        else:
            q_next, k_next = qn_ref[...], kn_ref[...]
        v_ones = jnp.concatenate([v_ref[:, head], jnp.ones((v_ref.shape[0], V_DIM), _BF16)], axis=1)
        maps = []
        for m in range(2):
            _store_scores(q_next, k_next, *scores[(i + 1) % 2], m)
            maps.append(_softmax_pv(*scores[i % 2], v_ones, m))
        o = maps[0] - lam * maps[1]
        o_ref[rows, head] = ((o * _rms(o)) * out_gain).astype(o_ref.dtype)


def _attention(q, k, v, lq, lk, sg, *, lambda_init, batch, seq, round_weights=()):
    n = q.shape[0]
    q_tiles = seq // Q_TILE
    assert (q_tiles * ATTN_HEADS_PER_STEP) % 2 == 0
    groups = N_ATTN_HEADS // ATTN_HEADS_PER_STEP
    steps = batch * groups

    def group_block(t):
        return t // groups, t % groups

    def next_step(t):
        return jnp.minimum(t + 1, steps - 1)

    def next_first_tile(t):
        b, g = group_block(next_step(t))
        return b * q_tiles, g * ATTN_HEADS_PER_STEP

    def next_first_head(t):
        b, g = group_block(next_step(t))
        return b, g * ATTN_HEADS_PER_STEP

    whole = pl.BlockSpec((seq, ATTN_HEADS_PER_STEP * V_DIM), group_block)
    small = lambda shape: pl.BlockSpec(shape, lambda t: (0, 0))
    scores = pltpu.VMEM((2, Q_TILE, seq), _F32)
    row_max = pltpu.VMEM((2, Q_TILE, 1), _F32)

    round_in, round_out, round_shapes = _round_specs(round_weights, steps)

    return pl.pallas_call(
        functools.partial(_attn_kernel, lambda_init=lambda_init, n_round=len(round_weights)),
        grid=(steps,),
        in_specs=[
            whole,
            pl.BlockSpec((Q_TILE, V_DIM), next_first_tile),
            whole,
            pl.BlockSpec((seq, V_DIM), next_first_head),
            whole,
            small((2, HEAD_DIM)), small((2, HEAD_DIM)), small((1, V_DIM)),
        ] + round_in,
        out_specs=[whole] + round_out,
        out_shape=[jax.ShapeDtypeStruct((n, D_ATTN), _BF16)] + round_shapes,
        scratch_shapes=[scores, scores, row_max, row_max],
        compiler_params=_params("arbitrary"),
    )(q, q, k, k, v, lq, lk, sg.reshape(1, V_DIM), *[w for w, _ in round_weights])


def _scan_direction(a_scr, b_scr, p_scr, h_scr, *, reverse):
    chains = [(lg, sg) for lg in range(LANE_GROUPS) for sg in range(SCAN_NSEG // SUBLANES)]

    def seg_rows(sg, j):
        return pl.ds(sg * SUBLANES * SCAN_PITCH + j, SUBLANES, stride=SCAN_PITCH)

    def step(jj, carry):
        j = (SCAN_SEG - 1 - jj) if reverse else jj
        hs, ps = carry
        new_h, new_p = [], []
        for c, (lg, sg) in enumerate(chains):
            a = a_scr[lg, seg_rows(sg, j), :]
            b = b_scr[lg, seg_rows(sg, j), :]
            h = a * hs[c] + b
            p = a * ps[c]
            h_scr[lg, seg_rows(sg, j), :] = h
            p_scr[lg, seg_rows(sg, j), :] = p
            new_h.append(h)
            new_p.append(p)
        return tuple(new_h), tuple(new_p)

    zeros = tuple(jnp.zeros((SUBLANES, LANES), _F32) for _ in chains)
    ones = tuple(jnp.ones((SUBLANES, LANES), _F32) for _ in chains)
    h_end, p_tot = lax.fori_loop(0, SCAN_SEG, step, (zeros, ones), unroll=8)

    order = range(SCAN_NSEG - 1, -1, -1) if reverse else range(SCAN_NSEG)
    carries = []
    for lg in range(LANE_GROUPS):
        carry_in = jnp.zeros((1, LANES), _F32)
        per_seg = [None] * SCAN_NSEG
        for s in order:
            c = chains.index((lg, s // SUBLANES))
            r = s % SUBLANES
            per_seg[s] = carry_in
            carry_in = p_tot[c][r:r + 1, :] * carry_in + h_end[c][r:r + 1, :]
        carries.append(per_seg)
    return carries


def _depthwise_conv(x_ref, cw, cb):
    seq = x_ref.shape[0]

    def taps(x, m2, m1, p1):
        return cb + cw[2:3, :] * x + cw[0:1, :] * m2 + cw[1:2, :] * m1 + cw[3:4, :] * p1

    row = lax.broadcasted_iota(jnp.int32, (SUBLANES, x_ref.shape[1]), 0)
    lo, hi = SUBLANES, seq - SUBLANES
    x_top = x_ref[0:lo, :]
    x_bot = x_ref[hi:seq, :]
    return jnp.concatenate([
        taps(x_top,
             jnp.where(row >= 2, pltpu.roll(x_top, 2, axis=0), 0.0),
             jnp.where(row >= 1, pltpu.roll(x_top, 1, axis=0), 0.0),
             x_ref[1:lo + 1, :]),
        taps(x_ref[lo:hi, :], x_ref[lo - 2:hi - 2, :], x_ref[lo - 1:hi - 1, :], x_ref[lo + 1:hi + 1, :]),
        taps(x_bot, x_ref[hi - 2:seq - 2, :], x_ref[hi - 1:seq - 1, :],
             jnp.where(row < SUBLANES - 1, pltpu.roll(x_bot, SUBLANES - 1, axis=0), 0.0)),
    ], axis=0)


def _lru_kernel(xr_ref, gy_ref, cw_ref, cb_ref, wg_ref, bg_ref, lam_ref, o_ref,
                a_scr, b_scr, p_scr, h_scr, sum_scr):
    xc = _depthwise_conv(xr_ref, cw_ref[...], cb_ref[...])
    xcb = xc.astype(_BF16)
    xc_half = 0.5 * xc

    for d in range(2):
        lam = lam_ref[d:d + 1, :]
        half_c = (-0.5 * LRU_C) * (jnp.maximum(-lam, 0.0) + jnp.log1p(jnp.exp(-jnp.abs(lam))))
        for half in range(D_LRU // MXU_DIM):
            cols = slice(half * MXU_DIM, (half + 1) * MXU_DIM)
            xh = xcb[:, cols]
            gr = jnp.dot(xh, wg_ref[d, 0, half], preferred_element_type=_F32)
            gi = jnp.dot(xh, wg_ref[d, 1, half], preferred_element_type=_F32)
            t_r = jnp.tanh(gr + 0.5 * bg_ref[d, 0:1, cols])
            t_i = jnp.tanh(gi + 0.5 * bg_ref[d, 1:2, cols])
            log_a = half_c[:, cols] * t_r + half_c[:, cols]
            a = jnp.exp(log_a)
            one_minus_a2 = jnp.tanh(log_a) * (-1.0 - a * a)
            mult = jnp.where(one_minus_a2 > 0.0, one_minus_a2 * lax.rsqrt(one_minus_a2), 0.0)
            b = mult * ((t_i + 1.0) * xc_half[:, cols])
            for sub in range(MXU_DIM // LANES):
                lg = half * (MXU_DIM // LANES) + sub
                lanes = slice(sub * LANES, (sub + 1) * LANES)
                for s in range(SCAN_NSEG):
                    src = slice(s * SCAN_SEG, (s + 1) * SCAN_SEG)
                    dst = pl.ds(s * SCAN_PITCH, SCAN_SEG)
                    a_scr[lg, dst, :] = a[src, lanes]
                    b_scr[lg, dst, :] = b[src, lanes]
        carries = _scan_direction(a_scr, b_scr, p_scr, h_scr, reverse=(d == 1))
        for lg in range(LANE_GROUPS):
            for s in range(SCAN_NSEG):
                rows = pl.ds(s * SCAN_PITCH, SCAN_SEG)
                dst = (slice(s * SCAN_SEG, (s + 1) * SCAN_SEG), slice(lg * LANES, (lg + 1) * LANES))
                hseg = h_scr[lg, rows, :] + p_scr[lg, rows, :] * carries[lg][s]
                if d == 0:
                    sum_scr[dst] = hseg
                else:
                    o_ref[dst] = ((sum_scr[dst] + hseg) * gy_ref[dst]).astype(o_ref.dtype)


def _recurrent_group(xr, gy, conv_w, conv_b, wg_bd, b_gate, lam, *, batch, seq):
    n = xr.shape[0]
    tok = pl.BlockSpec((seq, D_LRU), lambda b: (b, 0))
    scan_buf = pltpu.VMEM((LANE_GROUPS, SCAN_NSEG * SCAN_PITCH, LANES), _F32)
    return pl.pallas_call(
        _lru_kernel,
        grid=(batch,),
        in_specs=[
            tok, tok,
            pl.BlockSpec((CONV_WIDTH, D_LRU), lambda b: (0, 0)),
            pl.BlockSpec((1, D_LRU), lambda b: (0, 0)),
            pl.BlockSpec(wg_bd.shape, lambda b: (0, 0, 0, 0, 0)),
            pl.BlockSpec((2, 2, D_LRU), lambda b: (0, 0, 0)),
            pl.BlockSpec((2, D_LRU), lambda b: (0, 0)),
        ],
        out_specs=tok,
        out_shape=jax.ShapeDtypeStruct((n, D_LRU), _BF16),
        scratch_shapes=[scan_buf] * 4 + [pltpu.VMEM((seq, D_LRU), _F32)],
        compiler_params=_params("parallel"),
    )(xr, gy, conv_w, conv_b.reshape(1, D_LRU), wg_bd, b_gate, lam)


def _block_diag_gates(w_gate):
    per_tile = MXU_DIM // LRU_BLOCK
    w = w_gate.reshape(2, 2, N_LRU_BLOCKS // per_tile, per_tile, LRU_BLOCK, LRU_BLOCK)
    eye = jnp.eye(per_tile, dtype=w.dtype)
    bd = jnp.einsum('dghnij,nm->dghnimj', w, eye)
    return bd.reshape(2, 2, N_LRU_BLOCKS // per_tile, MXU_DIM, MXU_DIM)


def kernel(x, c, positions, w_ada, b_ada, norm_g, ffn1_w_in, ffn1_w_out, ffn2_w_in, ffn2_w_out,
           w_mix_in, w_mix_out, lambda_q, lambda_k, subln_g, conv_w, conv_b,
           lru_w_gate, lru_b_gate, lru_lambda):
    batch, seq, d = x.shape
    depth = w_ada.shape[0]
    assert (d, seq % TOKEN_TILE, seq % Q_TILE) == (D_MODEL, 0, 0)
    assert seq == SCAN_SEG * SCAN_NSEG

    ada = _ada_all_layers(c, w_ada, b_ada).reshape(depth, batch, 3 * N_SUB, d)
    cos_t, sin_t, ffn1_in, ffn1_out = _rope_tables(positions, [(ffn1_w_in, 0), (ffn1_w_out, 0)])
    x2 = x.reshape(batch * seq, d)

    for l in range(depth):
        lambda_init = 0.8 - 0.6 * math.exp(-0.3 * l)
        x2, ffn2_in, ffn2_out = _ffn_sublayer(x2, ada, norm_g, ffn1_in, ffn1_out, l=l, j=0, res_w=0.5, seq=seq,
                                              round_weights=[(ffn2_w_in, l), (ffn2_w_out, l)])
        q, k, v, xr, gy = _mix_in(x2, ada, norm_g, w_mix_in, cos_t, sin_t, l=l, seq=seq)
        later = [(ffn1_w_in, l + 1), (ffn1_w_out, l + 1)] if l + 1 < depth else []
        attn, *next_ffn1 = _attention(q, k, v, lambda_q[l], lambda_k[l], subln_g[l],
                                      lambda_init=lambda_init, batch=batch, seq=seq, round_weights=later)
        if next_ffn1:
            ffn1_in, ffn1_out = next_ffn1
        rec = _recurrent_group(xr, gy, conv_w[l], conv_b[l],
                               (0.5 * _block_diag_gates(lru_w_gate[l])).astype(_BF16),
                               lru_b_gate[l], lru_lambda[l], batch=batch, seq=seq)
        x2 = _ffn_sublayer(x2, ada, norm_g, ffn2_in, ffn2_out, l=l, j=2, res_w=0.5, seq=seq,
                           mixer=(attn, rec, w_mix_out))
    return x2.reshape(batch, seq, d)
```
